```python
import math
import jax, jax.numpy as jnp
from jax import lax
import numpy as np

D_MODEL = 1024
BATCH = 32
SEQ = 2048
DEPTH = 2

N_MEM = 256
HEAD_DIM = 64
SB_HEADS = 6
DSA_HEADS = 6
MEM_HEADS = 4
SB_W = SB_HEADS * HEAD_DIM
DSA_W = DSA_HEADS * HEAD_DIM
MEM_W = MEM_HEADS * HEAD_DIM
MIX_W = SB_W + DSA_W + MEM_W
KV_RANK = 128
IDX_HEADS = 8
IDX_DIM = 32
TOPK_MAX = 256
N_BUCKETS = 32
MAX_DISTANCE = 128
BLOCK_Q = 128
RMS_EPS = 1e-6
IN_SPLITS = (SB_W, SB_W, SB_W, SB_W,
             DSA_W, KV_RANK, DSA_W, IDX_HEADS * IDX_DIM, IDX_DIM, IDX_HEADS,
             MEM_W, MEM_W)
IN_COLS = sum(IN_SPLITS)

kernel_name = "hymba_sb_dsa_mem_hybrid"


def rmsnorm(x, g):
    xf = x.astype(jnp.float32)
    y = xf * lax.rsqrt(jnp.mean(xf * xf, axis=-1, keepdims=True) + RMS_EPS)
    return (y * g.astype(jnp.float32)).astype(x.dtype)


def t5_causal_bucket(rel):
    n = jnp.maximum(rel, 0)
    max_exact = N_BUCKETS // 2
    nf = jnp.maximum(n, 1).astype(jnp.float32)
    large = max_exact + (jnp.log(nf / max_exact) / math.log(MAX_DISTANCE / max_exact)
                         * (N_BUCKETS - max_exact)).astype(jnp.int32)
    large = jnp.minimum(large, N_BUCKETS - 1)
    return jnp.where(n < max_exact, n, large)


def stick_breaking_attention(q, k, v):
    S = q.shape[1]
    scale = q.shape[-1] ** -0.5
    outs = []
    for start in range(0, S, BLOCK_Q):
        end = start + BLOCK_Q
        z = jnp.einsum('bqhd,bkhd->bhqk', q[:, start:end], k[:, :end]).astype(jnp.float32) * scale
        t_pos = jnp.arange(start, end)[:, None]
        s_pos = jnp.arange(end)[None, :]
        causal = s_pos < t_pos
        log_1m = jnp.where(causal, jax.nn.log_sigmoid(-z), 0.0)
        suffix = lax.cumsum(log_1m, axis=3, reverse=True) - log_1m
        a = jnp.where(causal, jnp.exp(jax.nn.log_sigmoid(z) + suffix), 0.0)
        outs.append(jnp.einsum('bhqk,bkhd->bqhd', a.astype(v.dtype), v[:, :end]))
    return jnp.concatenate(outs, axis=1)


def dsa_sparse_attention(q, c_kv, w_uk, w_uv, iq, ik, iw, rel_bias, topk):
    S = q.shape[1]
    scale = q.shape[-1] ** -0.5
    q_lat = jnp.einsum('bshd,rhd->bshr', q, w_uk)
    gather = jax.vmap(lambda c, i: c[i])
    outs = []
    for start in range(0, S, BLOCK_Q):
        end = start + BLOCK_Q
        kv_len = min(S, max(end, topk))
        t_pos = jnp.arange(start, end)
        s_pos = jnp.arange(kv_len)
        dots = jnp.einsum('bqhd,bkd->bqhk', iq[:, start:end], ik[:, :kv_len])
        score = jnp.einsum('bqh,bqhk->bqk', iw[:, start:end], jax.nn.relu(dots)).astype(jnp.float32)
        score = jnp.where(s_pos[None, None, :] <= t_pos[None, :, None], score, -jnp.inf)
        _, idx = lax.top_k(score, topk)
        rel = t_pos[None, :, None] - idx
        valid = rel >= 0
        c_sel = gather(c_kv[:, :kv_len], idx)
        logits = jnp.einsum('bqhr,bqkr->bhqk', q_lat[:, start:end], c_sel).astype(jnp.float32) * scale
        bias = rel_bias.astype(jnp.float32)[t5_causal_bucket(rel)]
        logits = logits + jnp.transpose(bias, (0, 3, 1, 2))
        logits = jnp.where(valid[:, None], logits, -jnp.inf)
        p = jax.nn.softmax(logits, axis=-1)
        o_lat = jnp.einsum('bhqk,bqkr->bqhr', p.astype(c_sel.dtype), c_sel)
        outs.append(jnp.einsum('bqhr,rhd->bqhd', o_lat, w_uv))
    return jnp.concatenate(outs, axis=1)


def memory_attention(q, mem_k, mem_v):
    logits = jnp.einsum('bshd,bmhd->bhsm', q, mem_k).astype(jnp.float32) * (q.shape[-1] ** -0.5)
    p = jax.nn.softmax(logits, axis=-1)
    return jnp.einsum('bhsm,bmhd->bshd', p.astype(mem_v.dtype), mem_v)


def hybrid_layer(x, mem, pre_g, post_g, w_in, w_uk, w_uv, kv_g, w_mem_kv, w_out, rel_bias, topk):
    B, S, _ = x.shape
    M = mem.shape[1]
    h = rmsnorm(x, pre_g)
    proj = h @ w_in
    split_points = [int(p) for p in np.cumsum(IN_SPLITS)[:-1]]
    (sb_q, sb_k, sb_v, sb_gate,
     dsa_q, dsa_ckv, dsa_gate, idx_q, idx_k, idx_w,
     mem_q, mem_gate) = jnp.split(proj, split_points, axis=-1)
    heads = lambda t, n: t.reshape(B, S, n, HEAD_DIM)

    sb = stick_breaking_attention(heads(sb_q, SB_HEADS), heads(sb_k, SB_HEADS), heads(sb_v, SB_HEADS))
    sb = sb.reshape(B, S, SB_W) * jax.nn.silu(sb_gate)

    c_kv = rmsnorm(dsa_ckv, kv_g)
    iq = idx_q.reshape(B, S, IDX_HEADS, IDX_DIM)
    iw = idx_w * ((IDX_HEADS * IDX_DIM) ** -0.5)
    ds = dsa_sparse_attention(heads(dsa_q, DSA_HEADS), c_kv, w_uk, w_uv, iq, idx_k, iw, rel_bias, topk)
    ds = ds.reshape(B, S, DSA_W) * jax.nn.silu(dsa_gate)

    mem_k, mem_v = jnp.split(mem @ w_mem_kv, 2, axis=-1)
    mo = memory_attention(heads(mem_q, MEM_HEADS),
                          mem_k.reshape(B, M, MEM_HEADS, HEAD_DIM),
                          mem_v.reshape(B, M, MEM_HEADS, HEAD_DIM))
    mo = mo.reshape(B, S, MEM_W) * jax.nn.silu(mem_gate)

    y = jnp.concatenate([sb, ds, mo], axis=-1) @ w_out
    return x + rmsnorm(y, post_g)


def setup_inputs(seed: int = 0) -> dict:
    key = jax.random.key(seed)
    ks = jax.random.split(key, 12)
    f32 = jnp.float32
    x = jax.random.normal(ks[0], (BATCH, SEQ, D_MODEL), f32)
    mem = jax.random.normal(ks[1], (BATCH, N_MEM, D_MODEL), f32)
    pre_norm_g = 1.0 + 0.05 * jax.random.normal(ks[2], (DEPTH, D_MODEL), f32)
    post_norm_g = 1.0 + 0.05 * jax.random.normal(ks[3], (DEPTH, D_MODEL), f32)
    w_in = jax.random.normal(ks[4], (DEPTH, D_MODEL, IN_COLS), f32) * D_MODEL ** -0.5
    w_uk = jax.random.normal(ks[5], (DEPTH, KV_RANK, DSA_HEADS, HEAD_DIM), f32) * KV_RANK ** -0.5
    w_uv = jax.random.normal(ks[6], (DEPTH, KV_RANK, DSA_HEADS, HEAD_DIM), f32) * KV_RANK ** -0.5
    kv_norm_g = 1.0 + 0.05 * jax.random.normal(ks[7], (DEPTH, KV_RANK), f32)
    w_mem_kv = jax.random.normal(ks[8], (DEPTH, D_MODEL, 2 * MEM_W), f32) * D_MODEL ** -0.5
    w_out = jax.random.normal(ks[9], (DEPTH, MIX_W, D_MODEL), f32) * MIX_W ** -0.5
    rel_bias = 0.5 * jax.random.normal(ks[10], (N_BUCKETS, DSA_HEADS), f32)
    return {"x": x, "mem": mem, "pre_norm_g": pre_norm_g, "post_norm_g": post_norm_g,
            "w_in": w_in, "w_uk": w_uk, "w_uv": w_uv, "kv_norm_g": kv_norm_g,
            "w_mem_kv": w_mem_kv, "w_out": w_out, "rel_bias": rel_bias}


def reference(x, mem, pre_norm_g, post_norm_g, w_in, w_uk, w_uv, kv_norm_g, w_mem_kv, w_out, rel_bias):
    seq = x.shape[1]
    topk = min(TOPK_MAX, seq // 4)
    for layer in range(DEPTH):
        x = hybrid_layer(x, mem, pre_norm_g[layer], post_norm_g[layer], w_in[layer],
                         w_uk[layer], w_uv[layer], kv_norm_g[layer], w_mem_kv[layer],
                         w_out[layer], rel_bias, topk)
    return x
```

```python
import functools
import math

import jax
import jax.numpy as jnp
from jax import lax
from jax.experimental import pallas as pl
from jax.experimental.pallas import tpu as pltpu

D_MODEL = 1024
N_MEM = 256
HEAD_DIM = 64
SB_HEADS = 6
DSA_HEADS = 6
MEM_HEADS = 4
SB_W = SB_HEADS * HEAD_DIM
DSA_W = DSA_HEADS * HEAD_DIM
MEM_W = MEM_HEADS * HEAD_DIM
MIX_W = SB_W + DSA_W + MEM_W
KV_RANK = 128
IDX_HEADS = 8
IDX_DIM = 32
TOPK_MAX = 256
N_BUCKETS = 32
MAX_DISTANCE = 128
RMS_EPS = 1e-6

LANES = 128
ATTN_SCALE = HEAD_DIM ** -0.5
IDX_SCALE = (IDX_HEADS * IDX_DIM) ** -0.5
INT_MIN = -2 ** 31
NEG_BIG = -1e30
VMEM_LIMIT_BYTES = 56 * 1024 * 1024

OFF_SBQ = 0
OFF_SBK = OFF_SBQ + SB_W
OFF_SBV = OFF_SBK + SB_W
OFF_DSAQ = OFF_SBV + SB_W
OFF_CKV = OFF_DSAQ + DSA_W
OFF_IQ = OFF_CKV + KV_RANK
OFF_IKW = OFF_IQ + IDX_HEADS * IDX_DIM
OFF_MEMQ = OFF_IKW + LANES
OFF_GATE = OFF_MEMQ + MEM_W
PACKED_COLS = OFF_GATE + MIX_W

f32 = jnp.float32
bf16 = jnp.bfloat16


def _pack_w_in(w):
    o = 0
    parts = {}
    for name, n in (("sbq", SB_W), ("sbk", SB_W), ("sbv", SB_W), ("sbg", SB_W), ("dsaq", DSA_W),
                    ("ckv", KV_RANK), ("dsag", DSA_W), ("iq", IDX_HEADS * IDX_DIM), ("ik", IDX_DIM),
                    ("iw", IDX_HEADS), ("memq", MEM_W), ("memg", MEM_W)):
        parts[name] = w[:, o:o + n]
        o += n
    pad = jnp.zeros((w.shape[0], LANES - IDX_DIM - IDX_HEADS), w.dtype)
    packed = jnp.concatenate(
        [parts["sbq"], parts["sbk"], parts["sbv"], parts["dsaq"], parts["ckv"], parts["iq"],
         parts["ik"], parts["iw"], pad, parts["memq"], parts["sbg"], parts["dsag"], parts["memg"]], axis=1)
    assert packed.shape[1] == PACKED_COLS
    return packed.astype(bf16)


def _proj_kernel(x_ref, g_ref, w_ref, wuk_ref, kvg_ref,
                 sbq_ref, sbk_ref, sbv_ref, qlat_ref, ckv_ref, iq_ref, ikw_ref, memq_ref, gate_ref):
    x = x_ref[...]
    ms = jnp.mean(x * x, axis=-1, keepdims=True)
    h = (x * lax.rsqrt(ms + RMS_EPS) * g_ref[...]).astype(bf16)

    def seg(off, n):
        return jnp.dot(h, w_ref[:, off:off + n], preferred_element_type=f32)

    a = seg(OFF_SBQ, 3 * SB_W)
    for hd in range(SB_HEADS):
        lo = hd * HEAD_DIM
        sbq_ref[hd] = (a[:, lo:lo + HEAD_DIM] * ATTN_SCALE).astype(bf16)
        sbk_ref[hd] = a[:, SB_W + lo:SB_W + lo + HEAD_DIM].astype(bf16)
        sbv_ref[hd] = a[:, 2 * SB_W + lo:2 * SB_W + lo + HEAD_DIM].astype(bf16)

    dq = seg(OFF_DSAQ, DSA_W)
    for hd in range(DSA_HEADS):
        q = dq[:, hd * HEAD_DIM:(hd + 1) * HEAD_DIM].astype(bf16)
        ql = jnp.dot(q, wuk_ref[hd], preferred_element_type=f32)
        qlat_ref[hd] = (ql * ATTN_SCALE).astype(bf16)

    c = seg(OFF_CKV, KV_RANK)
    cms = jnp.mean(c * c, axis=-1, keepdims=True)
    cn = (c * lax.rsqrt(cms + RMS_EPS) * kvg_ref[...]).astype(bf16)
    ckv_ref[...] = jnp.concatenate([cn, jnp.ones_like(cn)], axis=1)

    e = seg(OFF_IQ, IDX_HEADS * IDX_DIM)
    for hd in range(IDX_HEADS):
        iq_ref[hd] = e[:, hd * IDX_DIM:(hd + 1) * IDX_DIM].astype(bf16)

    ikw_ref[...] = seg(OFF_IKW, LANES)

    mq = seg(OFF_MEMQ, MEM_W)
    for hd in range(MEM_HEADS):
        memq_ref[hd] = (mq[:, hd * HEAD_DIM:(hd + 1) * HEAD_DIM] * ATTN_SCALE).astype(bf16)

    gt = seg(OFF_GATE, MIX_W)
    gate_ref[...] = gt * jax.nn.sigmoid(gt)


def _project(x, pre_g, w_packed, wuk_t, kv_g, tm):
    B, S, D = x.shape
    grid = (B, S // tm)
    row = lambda b, i: (b, i, 0)
    head = lambda b, i: (b, 0, i, 0)
    const2 = lambda b, i: (0, 0)
    const3 = lambda b, i: (0, 0, 0)
    out_shape = (
        jax.ShapeDtypeStruct((B, SB_HEADS, S, HEAD_DIM), bf16),
        jax.ShapeDtypeStruct((B, SB_HEADS, S, HEAD_DIM), bf16),
        jax.ShapeDtypeStruct((B, SB_HEADS, S, HEAD_DIM), bf16),
        jax.ShapeDtypeStruct((B, DSA_HEADS, S, KV_RANK), bf16),
        jax.ShapeDtypeStruct((B, S, 2 * KV_RANK), bf16),
        jax.ShapeDtypeStruct((B, IDX_HEADS, S, IDX_DIM), bf16),
        jax.ShapeDtypeStruct((B, S, LANES), f32),
        jax.ShapeDtypeStruct((B, MEM_HEADS, S, HEAD_DIM), bf16),
        jax.ShapeDtypeStruct((B, S, MIX_W), f32),
    )
    out_specs = (
        pl.BlockSpec((None, SB_HEADS, tm, HEAD_DIM), head),
        pl.BlockSpec((None, SB_HEADS, tm, HEAD_DIM), head),
        pl.BlockSpec((None, SB_HEADS, tm, HEAD_DIM), head),
        pl.BlockSpec((None, DSA_HEADS, tm, KV_RANK), head),
        pl.BlockSpec((None, tm, 2 * KV_RANK), row),
        pl.BlockSpec((None, IDX_HEADS, tm, IDX_DIM), head),
        pl.BlockSpec((None, tm, LANES), row),
        pl.BlockSpec((None, MEM_HEADS, tm, HEAD_DIM), head),
        pl.BlockSpec((None, tm, MIX_W), row),
    )
    in_specs = [
        pl.BlockSpec((None, tm, D), row),
        pl.BlockSpec((1, D), const2),
        pl.BlockSpec((D, PACKED_COLS), const2),
        pl.BlockSpec((DSA_HEADS, HEAD_DIM, KV_RANK), const3),
        pl.BlockSpec((1, KV_RANK), const2),
    ]
    return pl.pallas_call(
        _proj_kernel, grid=grid, in_specs=in_specs, out_specs=out_specs, out_shape=out_shape,
        name="in_proj",
        compiler_params=pltpu.CompilerParams(
            dimension_semantics=("arbitrary", "arbitrary"), vmem_limit_bytes=VMEM_LIMIT_BYTES),
    )(x, pre_g.reshape(1, D), w_packed, wuk_t, kv_g.reshape(1, KV_RANK))


def _memkv_kernel(mem_ref, w_ref, k_ref, v_ref):
    m = mem_ref[...].astype(bf16)
    kv = jnp.dot(m, w_ref[...], preferred_element_type=f32)
    for hd in range(MEM_HEADS):
        lo = hd * HEAD_DIM
        k_ref[hd] = kv[:, lo:lo + HEAD_DIM].astype(bf16)
        v_ref[hd] = kv[:, MEM_W + lo:MEM_W + lo + HEAD_DIM].astype(bf16)


def _mem_kv(mem, w_mem_kv_bf):
    B, M, D = mem.shape
    out_shape = (jax.ShapeDtypeStruct((B, MEM_HEADS, M, HEAD_DIM), bf16),
                 jax.ShapeDtypeStruct((B, MEM_HEADS, M, HEAD_DIM), bf16))
    spec = pl.BlockSpec((None, MEM_HEADS, M, HEAD_DIM), lambda b: (b, 0, 0, 0))
    return pl.pallas_call(
        _memkv_kernel, grid=(B,),
        in_specs=[pl.BlockSpec((None, M, D), lambda b: (b, 0, 0)),
                  pl.BlockSpec((D, 2 * MEM_W), lambda b: (0, 0))],
        out_specs=(spec, spec), out_shape=out_shape, name="mem_kv",
        compiler_params=pltpu.CompilerParams(
            dimension_semantics=("arbitrary",), vmem_limit_bytes=VMEM_LIMIT_BYTES),
    )(mem, w_mem_kv_bf)


def _t5_bucket(n):
    max_exact = N_BUCKETS // 2
    nf = jnp.maximum(n, 1).astype(f32)
    large = max_exact + (jnp.log(nf / max_exact) / math.log(MAX_DISTANCE / max_exact)
                         * (N_BUCKETS - max_exact)).astype(jnp.int32)
    large = jnp.minimum(large, N_BUCKETS - 1)
    return jnp.where(n < max_exact, n, large)


def _attn_kernel(relb_ref, x_ref, sbq_ref, sbk_ref, sbv_ref, qlat_ref, ckv_ref, iq_ref, ikwq_ref, ikwk_ref,
                 memq_ref, memk_ref, memv_ref, gate_ref, wuv_ref, wout_ref, postg_ref,
                 out_ref,
                 keys_ref, bias_ref, tri_hi_ref, tri_lo_ref, m_ref, acc_ref, p_ref, sbacc_ref, mix_ref,
                 *, T, topk):
    b = pl.program_id(0)
    qi = pl.program_id(1)
    nchunk = qi + 1
    row_l = lax.broadcasted_iota(jnp.int32, (T, T), 0)
    col_l = lax.broadcasted_iota(jnp.int32, (T, T), 1)

    @pl.when((b == 0) & (qi == 0))
    def _init():
        tri_hi_ref[...] = jnp.where(row_l > col_l, 1.0, 0.0).astype(bf16)
        tri_lo_ref[...] = jnp.where(row_l < col_l, 1.0, 0.0).astype(bf16)
        for kind in range(3):
            dist = jnp.maximum(kind * T + row_l - col_l, 0)
            bucket = _t5_bucket(dist)
            for hd in range(DSA_HEADS):
                tile = jnp.zeros((T, T), f32)
                for k in range(N_BUCKETS):
                    tile = jnp.where(bucket == k, relb_ref[k, hd], tile)
                bias_ref[kind, hd] = tile

    iq2d = iq_ref[...].reshape(IDX_HEADS * T, IDX_DIM)
    iw = ikwq_ref[:, IDX_DIM:IDX_DIM + IDX_HEADS] * IDX_SCALE
    iw_cols = [iw[:, hd:hd + 1] for hd in range(IDX_HEADS)]

    def score_body(j, carry):
        ks = pl.multiple_of(j * T, T)
        ik = ikwk_ref[pl.ds(ks, T), :][:, :IDX_DIM].astype(bf16)
        dots = lax.dot_general(iq2d, ik, (((1,), (1,)), ((), ())), preferred_element_type=f32)
        score = jnp.zeros((T, T), f32)
        for hd in range(IDX_HEADS):
            score = score + iw_cols[hd] * jnp.maximum(dots[hd * T:(hd + 1) * T], 0.0)
        bits = pltpu.bitcast(score, jnp.int32)
        key = jnp.where(bits < 0, INT_MIN - bits, bits)
        valid = (ks + col_l) <= (qi * T + row_l)
        keys_ref[j] = jnp.where(valid, key, INT_MIN)
        return carry

    lax.fori_loop(0, nchunk, score_body, 0)

    def count_cmp(cand, strict):
        def body(j, acc):
            k = keys_ref[j]
            hit = (k > cand) if strict else (k >= cand)
            return acc + jnp.where(hit, 1.0, 0.0)
        acc = lax.fori_loop(0, nchunk, body, jnp.zeros((T, T), f32))
        return jnp.sum(acc, axis=1, keepdims=True)

    kf = float(topk)
    zero_col = jnp.zeros((T, 1), jnp.int32)
    thr0 = jnp.where(count_cmp(zero_col, False) >= kf, 0, INT_MIN).astype(jnp.int32)

    def bit_body(i, thr):
        cand = thr | (jnp.int32(1) << (30 - i))
        return jnp.where(count_cmp(cand, False) >= kf, cand, thr)

    thr = lax.fori_loop(0, 31, bit_body, thr0)
    need = kf - count_cmp(thr, True)

    m_ref[...] = jnp.full(m_ref.shape, NEG_BIG, f32)
    acc_ref[...] = jnp.zeros(acc_ref.shape, f32)
    qlat2d = qlat_ref[...].reshape(DSA_HEADS * T, KV_RANK)

    def dsa_body(j, eq_seen):
        ks = pl.multiple_of(j * T, T)
        k = keys_ref[j]
        eq = jnp.where(k == thr, 1.0, 0.0)
        prefix = jnp.dot(eq.astype(bf16), tri_lo_ref[...], preferred_element_type=f32) + eq_seen
        bump = jnp.where(prefix >= need, 1, 0).astype(jnp.int32)
        sel = k >= jnp.maximum(thr + bump, INT_MIN + 1)
        ckv = ckv_ref[pl.ds(ks, T), :]
        logits = lax.dot_general(qlat2d, ckv[:, :KV_RANK], (((1,), (1,)), ((), ())),
                                 preferred_element_type=f32)
        kind = jnp.minimum(qi - j, 2)
        for hd in range(DSA_HEADS):
            lg = jnp.where(sel, logits[hd * T:(hd + 1) * T] + bias_ref[kind, hd], NEG_BIG)
            m_old = m_ref[hd]
            m_new = jnp.maximum(m_old, jnp.max(lg, axis=1, keepdims=True))
            alpha = jnp.exp(m_old - m_new)
            p_ref[hd * T:(hd + 1) * T, :] = jnp.where(sel, jnp.exp(lg - m_new), 0.0).astype(bf16)
            acc_ref[hd * T:(hd + 1) * T, :] = acc_ref[hd * T:(hd + 1) * T, :] * alpha
            m_ref[hd] = m_new
        acc_ref[...] += jnp.dot(p_ref[...], ckv, preferred_element_type=f32)
        return eq_seen + jnp.sum(eq, axis=1, keepdims=True)

    lax.fori_loop(0, nchunk, dsa_body, jnp.zeros((T, 1), f32))

    for hd in range(DSA_HEADS):
        a = acc_ref[hd * T:(hd + 1) * T, :]
        o_lat = (a[:, :KV_RANK] / a[:, KV_RANK:]).astype(bf16)
        o = jnp.dot(o_lat, wuv_ref[hd], preferred_element_type=f32)
        mix_ref[:, SB_W + hd * HEAD_DIM:SB_W + (hd + 1) * HEAD_DIM] = o

    sbacc_ref[...] = jnp.zeros(sbacc_ref.shape, f32)

    def sb_body(i, carries):
        j = qi - i
        ks = pl.multiple_of(j * T, T)
        causal = (ks + col_l) < (qi * T + row_l)
        new = []
        for hd in range(SB_HEADS):
            z = lax.dot_general(sbq_ref[hd], sbk_ref[hd, pl.ds(ks, T), :], (((1,), (1,)), ((), ())),
                                preferred_element_type=f32)
            sp = jnp.log1p(jnp.exp(-jnp.abs(z)))
            log_beta = jnp.minimum(z, 0.0) - sp
            log_1m = jnp.where(causal, jnp.minimum(-z, 0.0) - sp, 0.0)
            hi = log_1m.astype(bf16)
            lo = (log_1m - hi.astype(f32)).astype(bf16)
            suffix = (jnp.dot(hi, tri_hi_ref[...], preferred_element_type=f32)
                      + jnp.dot(lo, tri_hi_ref[...], preferred_element_type=f32)) + carries[hd]
            a = jnp.where(causal, jnp.exp(log_beta + suffix), 0.0).astype(bf16)
            sbacc_ref[hd] += jnp.dot(a, sbv_ref[hd, pl.ds(ks, T), :], preferred_element_type=f32)
            new.append(carries[hd] + jnp.sum(log_1m, axis=1, keepdims=True))
        return tuple(new)

    lax.fori_loop(0, nchunk, sb_body, tuple(jnp.zeros((T, 1), f32) for _ in range(SB_HEADS)))
    for hd in range(SB_HEADS):
        mix_ref[:, hd * HEAD_DIM:(hd + 1) * HEAD_DIM] = sbacc_ref[hd]

    for hd in range(MEM_HEADS):
        lg = lax.dot_general(memq_ref[hd], memk_ref[hd], (((1,), (1,)), ((), ())), preferred_element_type=f32)
        e = jnp.exp(lg - jnp.max(lg, axis=1, keepdims=True))
        p = e / jnp.sum(e, axis=1, keepdims=True)
        o = jnp.dot(p.astype(bf16), memv_ref[hd], preferred_element_type=f32)
        lo = SB_W + DSA_W + hd * HEAD_DIM
        mix_ref[:, lo:lo + HEAD_DIM] = o

    gated = (mix_ref[...] * gate_ref[...]).astype(bf16)
    y = jnp.dot(gated, wout_ref[...], preferred_element_type=f32)
    ms = jnp.mean(y * y, axis=-1, keepdims=True)
    out_ref[...] = x_ref[...] + y * lax.rsqrt(ms + RMS_EPS) * postg_ref[...]


def _attention(x, proj, memk, memv, wuv_h, wout_bf, post_g, rel_bias, T, topk):
    sbq, sbk, sbv, qlat, ckv, iq, ikw, memq, gate = proj
    B, S, D = x.shape
    nq = S // T
    grid = (B, nq)
    qrow = lambda b, i: (b, i, 0)
    qhead = lambda b, i: (b, 0, i, 0)
    krow = lambda b, i: (b, 0, 0)
    khead = lambda b, i: (b, 0, 0, 0)
    const2 = lambda b, i: (0, 0)
    const3 = lambda b, i: (0, 0, 0)
    in_specs = [
        pl.BlockSpec(memory_space=pltpu.SMEM),
        pl.BlockSpec((None, T, D), qrow),
        pl.BlockSpec((None, SB_HEADS, T, HEAD_DIM), qhead),
        pl.BlockSpec((None, SB_HEADS, S, HEAD_DIM), khead),
        pl.BlockSpec((None, SB_HEADS, S, HEAD_DIM), khead),
        pl.BlockSpec((None, DSA_HEADS, T, KV_RANK), qhead),
        pl.BlockSpec((None, S, 2 * KV_RANK), krow),
        pl.BlockSpec((None, IDX_HEADS, T, IDX_DIM), qhead),
        pl.BlockSpec((None, T, LANES), qrow),
        pl.BlockSpec((None, S, LANES), krow),
        pl.BlockSpec((None, MEM_HEADS, T, HEAD_DIM), qhead),
        pl.BlockSpec((None, MEM_HEADS, N_MEM, HEAD_DIM), khead),
        pl.BlockSpec((None, MEM_HEADS, N_MEM, HEAD_DIM), khead),
        pl.BlockSpec((None, T, MIX_W), qrow),
        pl.BlockSpec((DSA_HEADS, KV_RANK, HEAD_DIM), const3),
        pl.BlockSpec((MIX_W, D), const2),
        pl.BlockSpec((1, D), const2),
    ]
    scratch = [
        pltpu.VMEM((nq, T, T), jnp.int32),
        pltpu.VMEM((3, DSA_HEADS, T, T), f32),
        pltpu.VMEM((T, T), bf16),
        pltpu.VMEM((T, T), bf16),
        pltpu.VMEM((DSA_HEADS, T, 1), f32),
        pltpu.VMEM((DSA_HEADS * T, 2 * KV_RANK), f32),
        pltpu.VMEM((DSA_HEADS * T, T), bf16),
        pltpu.VMEM((SB_HEADS, T, HEAD_DIM), f32),
        pltpu.VMEM((T, MIX_W), f32),
    ]
    return pl.pallas_call(
        functools.partial(_attn_kernel, T=T, topk=topk),
        grid=grid, in_specs=in_specs,
        out_specs=pl.BlockSpec((None, T, D), qrow),
        out_shape=jax.ShapeDtypeStruct((B, S, D), f32),
        scratch_shapes=scratch, name="hybrid_attn",
        compiler_params=pltpu.CompilerParams(
            dimension_semantics=("arbitrary", "arbitrary"), vmem_limit_bytes=VMEM_LIMIT_BYTES),
    )(rel_bias, x, sbq, sbk, sbv, qlat, ckv, iq, ikw, ikw, memq, memk, memv, gate,
      wuv_h, wout_bf, post_g.reshape(1, D))


def _tile_sizes(S):
    T = 256
    assert S % T == 0 and T >= MAX_DISTANCE
    tm = 512 if S % 512 == 0 else T
    return tm, T


def kernel(x, mem, pre_norm_g, post_norm_g, w_in, w_uk, w_uv, kv_norm_g, w_mem_kv, w_out, rel_bias):
    B, S, D = x.shape
    assert D == D_MODEL and mem.shape[1] == N_MEM
    topk = min(TOPK_MAX, S // 4)
    tm, T = _tile_sizes(S)
    for layer in range(w_in.shape[0]):
        w_packed = _pack_w_in(w_in[layer])
        wuk_t = jnp.transpose(w_uk[layer], (1, 2, 0)).astype(bf16)
        wuv_h = jnp.transpose(w_uv[layer], (1, 0, 2)).astype(bf16)
        proj = _project(x, pre_norm_g[layer], w_packed, wuk_t, kv_norm_g[layer], tm)
        memk, memv = _mem_kv(mem, w_mem_kv[layer].astype(bf16))
        x = _attention(x, proj, memk, memv, wuv_h, w_out[layer].astype(bf16), post_norm_g[layer],
                       rel_bias, T, topk)
    return x
```

```python
import functools
import math

import jax
import jax.numpy as jnp
from jax import lax
from jax.experimental import pallas as pl
from jax.experimental.pallas import tpu as pltpu

D_MODEL = 1024
N_MEM = 256
HEAD_DIM = 64
SB_HEADS = 6
DSA_HEADS = 6
MEM_HEADS = 4
SB_W = SB_HEADS * HEAD_DIM
DSA_W = DSA_HEADS * HEAD_DIM
MEM_W = MEM_HEADS * HEAD_DIM
MIX_W = SB_W + DSA_W + MEM_W
KV_RANK = 128
IDX_HEADS = 8
IDX_DIM = 32
TOPK_MAX = 256
N_BUCKETS = 32
MAX_DISTANCE = 128
RMS_EPS = 1e-6

LANES = 128
SUBLANES = 8
BF16_ROWS = 16
ATTN_SCALE = HEAD_DIM ** -0.5
IDX_SCALE = (IDX_HEADS * IDX_DIM) ** -0.5
INT_MIN = -2 ** 31
NEG_BIG = -1e30
EXP_ZERO_BELOW = -105.0
VMEM_LIMIT_BYTES = 56 * 1024 * 1024
KV_AUG = KV_RANK + BF16_ROWS

OFF_SBQ = 0
OFF_SBK = OFF_SBQ + SB_W
OFF_SBV = OFF_SBK + SB_W
OFF_DSAQ = OFF_SBV + SB_W
OFF_CKV = OFF_DSAQ + DSA_W
OFF_IQ = OFF_CKV + KV_RANK
OFF_IKW = OFF_IQ + IDX_HEADS * IDX_DIM
OFF_MEMQ = OFF_IKW + LANES
OFF_GATE = OFF_MEMQ + MEM_W
PACKED_COLS = OFF_GATE + MIX_W

f32 = jnp.float32
bf16 = jnp.bfloat16
NT = (((1,), (1,)), ((), ()))


def _pack_w_in(w):
    o = 0
    parts = {}
    for name, n in (("sbq", SB_W), ("sbk", SB_W), ("sbv", SB_W), ("sbg", SB_W), ("dsaq", DSA_W),
                    ("ckv", KV_RANK), ("dsag", DSA_W), ("iq", IDX_HEADS * IDX_DIM), ("ik", IDX_DIM),
                    ("iw", IDX_HEADS), ("memq", MEM_W), ("memg", MEM_W)):
        parts[name] = w[:, o:o + n]
        o += n
    pad = jnp.zeros((w.shape[0], LANES - IDX_DIM - IDX_HEADS), w.dtype)
    packed = jnp.concatenate(
        [parts["sbq"], parts["sbk"], parts["sbv"], parts["dsaq"], parts["ckv"], parts["iq"],
         parts["ik"], parts["iw"], pad, parts["memq"], parts["sbg"], parts["dsag"], parts["memg"]], axis=1)
    assert packed.shape[1] == PACKED_COLS
    return packed.astype(bf16)


def _proj_kernel(x_ref, g_ref, w_ref, wuk_ref, kvg_ref,
                 sbq_ref, sbk_ref, sbvt_ref, qlat_ref, ckv_ref, ckvt_ref, iq_ref, idxk_ref, iwt_ref,
                 memq_ref, gate_ref, *, T):
    tm = x_ref.shape[0]
    x = x_ref[...]
    ms = jnp.mean(x * x, axis=-1, keepdims=True)
    h = (x * lax.rsqrt(ms + RMS_EPS) * g_ref[...]).astype(bf16)

    def seg(off, n):
        return jnp.dot(h, w_ref[:, off:off + n], preferred_element_type=f32)

    a = seg(OFF_SBQ, 2 * SB_W)
    for hd in range(SB_HEADS):
        lo = hd * HEAD_DIM
        sbq_ref[hd] = (a[:, lo:lo + HEAD_DIM] * ATTN_SCALE).astype(bf16)
        sbk_ref[hd] = a[:, SB_W + lo:SB_W + lo + HEAD_DIM].astype(bf16)

    vt = seg(OFF_SBV, SB_W).T
    for r in range(tm // T):
        for hd in range(SB_HEADS):
            sbvt_ref[r, hd] = vt[hd * HEAD_DIM:(hd + 1) * HEAD_DIM, r * T:(r + 1) * T].astype(bf16)

    dq = seg(OFF_DSAQ, DSA_W)
    for hd in range(DSA_HEADS):
        q = dq[:, hd * HEAD_DIM:(hd + 1) * HEAD_DIM].astype(bf16)
        ql = jnp.dot(q, wuk_ref[hd], preferred_element_type=f32)
        qlat_ref[hd] = (ql * ATTN_SCALE).astype(bf16)

    c = seg(OFF_CKV, KV_RANK)
    cms = jnp.mean(c * c, axis=-1, keepdims=True)
    cn = c * lax.rsqrt(cms + RMS_EPS) * kvg_ref[...]
    ckv_ref[...] = cn.astype(bf16)
    cnt = cn.T
    for r in range(tm // T):
        ckvt_ref[r, :KV_RANK, :] = cnt[:, r * T:(r + 1) * T].astype(bf16)
        ckvt_ref[r, KV_RANK:, :] = jnp.ones((BF16_ROWS, T), bf16)

    e = seg(OFF_IQ, IDX_HEADS * IDX_DIM)
    for hd in range(IDX_HEADS):
        iq_ref[hd] = e[:, hd * IDX_DIM:(hd + 1) * IDX_DIM].astype(bf16)

    kw = seg(OFF_IKW, LANES)
    idxk_ref[...] = kw[:, :IDX_DIM].astype(bf16)
    iwt_ref[...] = kw.T[IDX_DIM:IDX_DIM + IDX_HEADS, :] * IDX_SCALE

    mq = seg(OFF_MEMQ, MEM_W)
    for hd in range(MEM_HEADS):
        memq_ref[hd] = (mq[:, hd * HEAD_DIM:(hd + 1) * HEAD_DIM] * ATTN_SCALE).astype(bf16)

    gt = seg(OFF_GATE, MIX_W)
    gate_ref[...] = gt * jax.nn.sigmoid(gt)


def _project(x, pre_g, w_packed, wuk_t, kv_g, tm, T):
    B, S, D = x.shape
    grid = (B, S // tm)
    rpt = tm // T
    row = lambda b, i: (b, i, 0)
    head = lambda b, i: (b, 0, i, 0)
    const2 = lambda b, i: (0, 0)
    const3 = lambda b, i: (0, 0, 0)
    out_shape = (
        jax.ShapeDtypeStruct((B, SB_HEADS, S, HEAD_DIM), bf16),
        jax.ShapeDtypeStruct((B, SB_HEADS, S, HEAD_DIM), bf16),
        jax.ShapeDtypeStruct((B, S // T, SB_HEADS, HEAD_DIM, T), bf16),
        jax.ShapeDtypeStruct((B, DSA_HEADS, S, KV_RANK), bf16),
        jax.ShapeDtypeStruct((B, S, KV_RANK), bf16),
        jax.ShapeDtypeStruct((B, S // T, KV_AUG, T), bf16),
        jax.ShapeDtypeStruct((B, IDX_HEADS, S, IDX_DIM), bf16),
        jax.ShapeDtypeStruct((B, S, IDX_DIM), bf16),
        jax.ShapeDtypeStruct((B, IDX_HEADS, S), f32),
        jax.ShapeDtypeStruct((B, MEM_HEADS, S, HEAD_DIM), bf16),
        jax.ShapeDtypeStruct((B, S, MIX_W), f32),
    )
    out_specs = (
        pl.BlockSpec((None, SB_HEADS, tm, HEAD_DIM), head),
        pl.BlockSpec((None, SB_HEADS, tm, HEAD_DIM), head),
        pl.BlockSpec((None, rpt, SB_HEADS, HEAD_DIM, T), lambda b, i: (b, i, 0, 0, 0)),
        pl.BlockSpec((None, DSA_HEADS, tm, KV_RANK), head),
        pl.BlockSpec((None, tm, KV_RANK), row),
        pl.BlockSpec((None, rpt, KV_AUG, T), lambda b, i: (b, i, 0, 0)),
        pl.BlockSpec((None, IDX_HEADS, tm, IDX_DIM), head),
        pl.BlockSpec((None, tm, IDX_DIM), row),
        pl.BlockSpec((None, IDX_HEADS, tm), lambda b, i: (b, 0, i)),
        pl.BlockSpec((None, MEM_HEADS, tm, HEAD_DIM), head),
        pl.BlockSpec((None, tm, MIX_W), row),
    )
    in_specs = [
        pl.BlockSpec((None, tm, D), row),
        pl.BlockSpec((1, D), const2),
        pl.BlockSpec((D, PACKED_COLS), const2),
        pl.BlockSpec((DSA_HEADS, HEAD_DIM, KV_RANK), const3),
        pl.BlockSpec((1, KV_RANK), const2),
    ]
    return pl.pallas_call(
        functools.partial(_proj_kernel, T=T),
        grid=grid, in_specs=in_specs, out_specs=out_specs, out_shape=out_shape,
        name="in_proj",
        compiler_params=pltpu.CompilerParams(
            dimension_semantics=("arbitrary", "arbitrary"), vmem_limit_bytes=VMEM_LIMIT_BYTES),
    )(x, pre_g.reshape(1, D), w_packed, wuk_t, kv_g.reshape(1, KV_RANK))


def _memkv_kernel(mem_ref, w_ref, k_ref, vt_ref):
    m = mem_ref[...].astype(bf16)
    kv = jnp.dot(m, w_ref[...], preferred_element_type=f32)
    vt = kv[:, MEM_W:].T
    for hd in range(MEM_HEADS):
        lo = hd * HEAD_DIM
        k_ref[hd] = kv[:, lo:lo + HEAD_DIM].astype(bf16)
        vt_ref[hd] = vt[lo:lo + HEAD_DIM, :].astype(bf16)


def _mem_kv(mem, w_mem_kv_bf):
    B, M, D = mem.shape
    out_shape = (jax.ShapeDtypeStruct((B, MEM_HEADS, M, HEAD_DIM), bf16),
                 jax.ShapeDtypeStruct((B, MEM_HEADS, HEAD_DIM, M), bf16))
    return pl.pallas_call(
        _memkv_kernel, grid=(B,),
        in_specs=[pl.BlockSpec((None, M, D), lambda b: (b, 0, 0)),
                  pl.BlockSpec((D, 2 * MEM_W), lambda b: (0, 0))],
        out_specs=(pl.BlockSpec((None, MEM_HEADS, M, HEAD_DIM), lambda b: (b, 0, 0, 0)),
                   pl.BlockSpec((None, MEM_HEADS, HEAD_DIM, M), lambda b: (b, 0, 0, 0))),
        out_shape=out_shape, name="mem_kv",
        compiler_params=pltpu.CompilerParams(
            dimension_semantics=("arbitrary",), vmem_limit_bytes=VMEM_LIMIT_BYTES),
    )(mem, w_mem_kv_bf)


def _t5_bucket(n):
    max_exact = N_BUCKETS // 2
    nf = jnp.maximum(n, 1).astype(f32)
    large = max_exact + (jnp.log(nf / max_exact) / math.log(MAX_DISTANCE / max_exact)
                         * (N_BUCKETS - max_exact)).astype(jnp.int32)
    large = jnp.minimum(large, N_BUCKETS - 1)
    return jnp.where(n < max_exact, n, large)


def _fold_keys(v, op):
    t = v.shape[0]
    v3 = v.reshape(t // SUBLANES, SUBLANES, v.shape[1])
    return op(v3, axis=0)


def _attn_kernel(relb_ref, x_ref, sbq_ref, sbk_ref, sbvt_ref, qlat_ref, ckv_ref, ckvt_ref, iq_ref, idxk_ref,
                 iwt_ref, memq_ref, memk_ref, memvt_ref, gate_ref, wuvt_ref, wout_ref, postg_ref,
                 out_ref,
                 keys_ref, lg_ref, bias_ref, tri_gt_ref, tri_lt_ref, acc_ref, sbacc_ref, mixt_ref,
                 *, T, topk):
    b = pl.program_id(0)
    qi = pl.program_id(1)
    nchunk = qi + 1
    key_l = lax.broadcasted_iota(jnp.int32, (T, T), 0)
    qry_l = lax.broadcasted_iota(jnp.int32, (T, T), 1)

    @pl.when((b == 0) & (qi == 0))
    def _init():
        tri_gt_ref[...] = jnp.where(qry_l > key_l, 1.0, 0.0).astype(bf16)
        tri_lt_ref[...] = jnp.where(qry_l < key_l, 1.0, 0.0).astype(bf16)
        for kind in range(3):
            dist = jnp.maximum(kind * T + qry_l - key_l, 0)
            bucket = _t5_bucket(dist)
            for hd in range(DSA_HEADS):
                tile = jnp.zeros((T, T), f32)
                for k in range(N_BUCKETS):
                    tile = jnp.where(bucket == k, relb_ref[k, hd], tile)
                bias_ref[kind, hd] = tile

    iq2d = iq_ref[...].reshape(IDX_HEADS * T, IDX_DIM)
    iwt = iwt_ref[...]

    def score_body(j, carry):
        ks = pl.multiple_of(j * T, T)
        dots = lax.dot_general(idxk_ref[pl.ds(ks, T), :], iq2d, NT, preferred_element_type=f32)
        score = jnp.zeros((T, T), f32)
        for hd in range(IDX_HEADS):
            score = score + iwt[hd:hd + 1, :] * jnp.maximum(dots[:, hd * T:(hd + 1) * T], 0.0)
        bits = pltpu.bitcast(score, jnp.int32)
        key = jnp.where(bits < 0, INT_MIN - bits, bits)
        valid = (ks + key_l) <= (qi * T + qry_l)
        keys_ref[j] = jnp.where(valid, key, INT_MIN)
        return carry

    lax.fori_loop(0, nchunk, score_body, 0)

    def count_cmp(cand, strict):
        def body(j, acc):
            k = keys_ref[j]
            hit = (k > cand) if strict else (k >= cand)
            return acc + _fold_keys(jnp.where(hit, 1.0, 0.0), jnp.sum)
        acc = lax.fori_loop(0, nchunk, body, jnp.zeros((SUBLANES, T), f32))
        return jnp.sum(acc, axis=0, keepdims=True)

    kf = float(topk)
    thr0 = jnp.where(count_cmp(jnp.zeros((1, T), jnp.int32), False) >= kf, 0, INT_MIN).astype(jnp.int32)

    def bit_body(i, thr):
        cand = thr | (jnp.int32(1) << (30 - i))
        return jnp.where(count_cmp(cand, False) >= kf, cand, thr)

    thr = lax.fori_loop(0, 31, bit_body, thr0)
    n_ge = count_cmp(thr, False)
    need = kf - count_cmp(thr, True)
    has_ties = jnp.max(jnp.where((n_ge > kf) & (thr > INT_MIN), 1.0, 0.0)) > 0.0
    thr_floor = jnp.maximum(thr, INT_MIN + 1)

    qlat2d = qlat_ref[...].reshape(DSA_HEADS * T, KV_RANK)

    def dsa_logits(j, carry):
        eq_seen, maxes = carry
        ks = pl.multiple_of(j * T, T)
        k = keys_ref[j]

        def tie_cut(_):
            eq = jnp.where(k == thr, 1.0, 0.0)
            prefix = jnp.dot(tri_lt_ref[...], eq.astype(bf16), preferred_element_type=f32) + eq_seen
            bump = jnp.where(prefix >= need, 1, 0).astype(jnp.int32)
            return (jnp.maximum(thr + bump, INT_MIN + 1),
                    eq_seen + jnp.sum(_fold_keys(eq, jnp.sum), axis=0, keepdims=True))

        def plain_cut(_):
            return jnp.broadcast_to(thr_floor, (T, T)), eq_seen

        cut, eq_seen = lax.cond(has_ties, tie_cut, plain_cut, 0)
        sel = k >= cut
        logits = lax.dot_general(ckv_ref[pl.ds(ks, T), :], qlat2d, NT, preferred_element_type=f32)
        kind = jnp.minimum(qi - j, 2)
        new_maxes = []
        for hd in range(DSA_HEADS):
            lg = jnp.where(sel, logits[:, hd * T:(hd + 1) * T] + bias_ref[kind, hd], NEG_BIG)
            lg_ref[j, :, hd * T:(hd + 1) * T] = lg
            new_maxes.append(jnp.maximum(maxes[hd], _fold_keys(lg, jnp.max)))
        return eq_seen, tuple(new_maxes)

    init_max = tuple(jnp.full((SUBLANES, T), NEG_BIG, f32) for _ in range(DSA_HEADS))
    _, maxes = lax.fori_loop(0, nchunk, dsa_logits, (jnp.zeros((1, T), f32), init_max))
    row_max = [jnp.max(m, axis=0, keepdims=True) for m in maxes]

    acc_ref[...] = jnp.zeros(acc_ref.shape, f32)

    def dsa_values(j, carry):
        ckvt = ckvt_ref[j]
        for hd in range(DSA_HEADS):
            p = jnp.exp(lg_ref[j, :, hd * T:(hd + 1) * T] - row_max[hd]).astype(bf16)
            acc_ref[hd] += jnp.dot(ckvt, p, preferred_element_type=f32)
        return carry

    lax.fori_loop(0, nchunk, dsa_values, 0)

    for hd in range(DSA_HEADS):
        a = acc_ref[hd]
        o_lat = (a[:KV_RANK] / a[KV_RANK:KV_RANK + 1]).astype(bf16)
        mixt_ref[SB_W + hd * HEAD_DIM:SB_W + (hd + 1) * HEAD_DIM, :] = jnp.dot(
            wuvt_ref[hd], o_lat, preferred_element_type=f32)

    sbacc_ref[...] = jnp.zeros(sbacc_ref.shape, f32)

    def sb_chunk(j, carries, masked):
        ks = pl.multiple_of(j * T, T)
        causal = key_l < qry_l
        new = []
        for hd in range(SB_HEADS):
            z = lax.dot_general(sbk_ref[hd, pl.ds(ks, T), :], sbq_ref[hd], NT, preferred_element_type=f32)
            sp = jnp.log1p(jnp.exp(-jnp.abs(z)))
            log_beta = jnp.minimum(z, 0.0) - sp
            log_1m = jnp.minimum(-z, 0.0) - sp
            if masked:
                log_1m = jnp.where(causal, log_1m, 0.0)
            hi = log_1m.astype(bf16)
            lo = (log_1m - hi.astype(f32)).astype(bf16)
            inner = (jnp.dot(tri_gt_ref[...], hi, preferred_element_type=f32)
                     + jnp.dot(tri_gt_ref[...], lo, preferred_element_type=f32))
            w = jnp.exp(log_beta + inner + carries[hd])
            if masked:
                w = jnp.where(causal, w, 0.0)
            sbacc_ref[hd] += jnp.dot(sbvt_ref[j, hd], w.astype(bf16), preferred_element_type=f32)
            new.append(carries[hd] + inner[0:1, :] + log_1m[0:1, :])
        return tuple(new)

    carries = sb_chunk(qi, tuple(jnp.zeros((1, T), f32) for _ in range(SB_HEADS)), True)

    def carry_max(cs):
        return jnp.max(functools.reduce(jnp.maximum, cs))

    def sb_cond(state):
        i, cmax, _ = state
        return (i < nchunk) & (cmax >= EXP_ZERO_BELOW)

    def sb_body(state):
        i, _, cs = state
        cs = sb_chunk(qi - i, cs, False)
        return i + 1, carry_max(cs), cs

    lax.while_loop(sb_cond, sb_body, (jnp.int32(1), carry_max(carries), carries))
    for hd in range(SB_HEADS):
        mixt_ref[hd * HEAD_DIM:(hd + 1) * HEAD_DIM, :] = sbacc_ref[hd]

    for hd in range(MEM_HEADS):
        lg = lax.dot_general(memk_ref[hd], memq_ref[hd], NT, preferred_element_type=f32)
        e = jnp.exp(lg - jnp.max(lg, axis=0, keepdims=True))
        p = e / jnp.sum(e, axis=0, keepdims=True)
        lo = SB_W + DSA_W + hd * HEAD_DIM
        mixt_ref[lo:lo + HEAD_DIM, :] = jnp.dot(memvt_ref[hd], p.astype(bf16), preferred_element_type=f32)

    gated = (mixt_ref[...].T * gate_ref[...]).astype(bf16)
    y = jnp.dot(gated, wout_ref[...], preferred_element_type=f32)
    ms = jnp.mean(y * y, axis=-1, keepdims=True)
    out_ref[...] = x_ref[...] + y * lax.rsqrt(ms + RMS_EPS) * postg_ref[...]


def _attention(x, proj, memk, memvt, wuvt_h, wout_bf, post_g, rel_bias, T, topk):
    sbq, sbk, sbvt, qlat, ckv, ckvt, iq, idxk, iwt, memq, gate = proj
    B, S, D = x.shape
    nq = S // T
    grid = (B, nq)
    qrow = lambda b, i: (b, i, 0)
    qhead = lambda b, i: (b, 0, i, 0)
    krow = lambda b, i: (b, 0, 0)
    khead = lambda b, i: (b, 0, 0, 0)
    const2 = lambda b, i: (0, 0)
    const3 = lambda b, i: (0, 0, 0)
    in_specs = [
        pl.BlockSpec(memory_space=pltpu.SMEM),
        pl.BlockSpec((None, T, D), qrow),
        pl.BlockSpec((None, SB_HEADS, T, HEAD_DIM), qhead),
        pl.BlockSpec((None, SB_HEADS, S, HEAD_DIM), khead),
        pl.BlockSpec((None, nq, SB_HEADS, HEAD_DIM, T), lambda b, i: (b, 0, 0, 0, 0)),
        pl.BlockSpec((None, DSA_HEADS, T, KV_RANK), qhead),
        pl.BlockSpec((None, S, KV_RANK), krow),
        pl.BlockSpec((None, nq, KV_AUG, T), khead),
        pl.BlockSpec((None, IDX_HEADS, T, IDX_DIM), qhead),
        pl.BlockSpec((None, S, IDX_DIM), krow),
        pl.BlockSpec((None, IDX_HEADS, T), lambda b, i: (b, 0, i)),
        pl.BlockSpec((None, MEM_HEADS, T, HEAD_DIM), qhead),
        pl.BlockSpec((None, MEM_HEADS, N_MEM, HEAD_DIM), khead),
        pl.BlockSpec((None, MEM_HEADS, HEAD_DIM, N_MEM), khead),
        pl.BlockSpec((None, T, MIX_W), qrow),
        pl.BlockSpec((DSA_HEADS, HEAD_DIM, KV_RANK), const3),
        pl.BlockSpec((MIX_W, D), const2),
        pl.BlockSpec((1, D), const2),
    ]
    scratch = [
        pltpu.VMEM((nq, T, T), jnp.int32),
        pltpu.VMEM((nq, T, DSA_HEADS * T), f32),
        pltpu.VMEM((3, DSA_HEADS, T, T), f32),
        pltpu.VMEM((T, T), bf16),
        pltpu.VMEM((T, T), bf16),
        pltpu.VMEM((DSA_HEADS, KV_AUG, T), f32),
        pltpu.VMEM((SB_HEADS, HEAD_DIM, T), f32),
        pltpu.VMEM((MIX_W, T), f32),
    ]
    return pl.pallas_call(
        functools.partial(_attn_kernel, T=T, topk=topk),
        grid=grid, in_specs=in_specs,
        out_specs=pl.BlockSpec((None, T, D), qrow),
        out_shape=jax.ShapeDtypeStruct((B, S, D), f32),
        scratch_shapes=scratch, name="hybrid_attn",
        compiler_params=pltpu.CompilerParams(
            dimension_semantics=("arbitrary", "arbitrary"), vmem_limit_bytes=VMEM_LIMIT_BYTES),
    )(rel_bias, x, sbq, sbk, sbvt, qlat, ckv, ckvt, iq, idxk, iwt, memq, memk, memvt, gate,
      wuvt_h, wout_bf, post_g.reshape(1, D))


def _tile_sizes(S):
    T = 256
    assert S % T == 0 and T >= MAX_DISTANCE
    tm = 512 if S % 512 == 0 else T
    return tm, T


def kernel(x, mem, pre_norm_g, post_norm_g, w_in, w_uk, w_uv, kv_norm_g, w_mem_kv, w_out, rel_bias):
    B, S, D = x.shape
    assert D == D_MODEL and mem.shape[1] == N_MEM
    topk = min(TOPK_MAX, S // 4)
    tm, T = _tile_sizes(S)
    for layer in range(w_in.shape[0]):
        w_packed = _pack_w_in(w_in[layer])
        wuk_t = jnp.transpose(w_uk[layer], (1, 2, 0)).astype(bf16)
        wuvt_h = jnp.transpose(w_uv[layer], (1, 2, 0)).astype(bf16)
        proj = _project(x, pre_norm_g[layer], w_packed, wuk_t, kv_norm_g[layer], tm, T)
        memk, memvt = _mem_kv(mem, w_mem_kv[layer].astype(bf16))
        x = _attention(x, proj, memk, memvt, wuvt_h, w_out[layer].astype(bf16), post_norm_g[layer],
                       rel_bias, T, topk)
    return x
```

```python
import functools
import math

import jax
import jax.numpy as jnp
from jax import lax
from jax.experimental import pallas as pl
from jax.experimental.pallas import tpu as pltpu

D_MODEL = 1024
N_MEM = 256
HEAD_DIM = 64
SB_HEADS = 6
DSA_HEADS = 6
MEM_HEADS = 4
SB_W = SB_HEADS * HEAD_DIM
DSA_W = DSA_HEADS * HEAD_DIM
MEM_W = MEM_HEADS * HEAD_DIM
MIX_W = SB_W + DSA_W + MEM_W
KV_RANK = 128
IDX_HEADS = 8
IDX_DIM = 32
TOPK_MAX = 256
N_BUCKETS = 32
MAX_DISTANCE = 128
RMS_EPS = 1e-6

LANES = 128
SUBLANES = 8
BF16_ROWS = 16
ATTN_SCALE = HEAD_DIM ** -0.5
IDX_SCALE = (IDX_HEADS * IDX_DIM) ** -0.5
INT_MIN = -2 ** 31
NEG_BIG = -1e30
EXP_ZERO_BELOW = -105.0
VMEM_LIMIT_BYTES = 56 * 1024 * 1024
KV_AUG = KV_RANK + BF16_ROWS

OFF_SBQ = 0
OFF_SBK = OFF_SBQ + SB_W
OFF_SBV = OFF_SBK + SB_W
OFF_DSAQ = OFF_SBV + SB_W
OFF_CKV = OFF_DSAQ + DSA_W
OFF_IQ = OFF_CKV + KV_RANK
OFF_IKW = OFF_IQ + IDX_HEADS * IDX_DIM
OFF_MEMQ = OFF_IKW + LANES
OFF_GATE = OFF_MEMQ + MEM_W
PACKED_COLS = OFF_GATE + MIX_W

f32 = jnp.float32
bf16 = jnp.bfloat16
NT = (((1,), (1,)), ((), ()))


def _pack_w_in(w):
    o = 0
    parts = {}
    for name, n in (("sbq", SB_W), ("sbk", SB_W), ("sbv", SB_W), ("sbg", SB_W), ("dsaq", DSA_W),
                    ("ckv", KV_RANK), ("dsag", DSA_W), ("iq", IDX_HEADS * IDX_DIM), ("ik", IDX_DIM),
                    ("iw", IDX_HEADS), ("memq", MEM_W), ("memg", MEM_W)):
        parts[name] = w[:, o:o + n]
        o += n
    pad = jnp.zeros((w.shape[0], LANES - IDX_DIM - IDX_HEADS), w.dtype)
    packed = jnp.concatenate(
        [parts["sbq"], parts["sbk"], parts["sbv"], parts["dsaq"], parts["ckv"], parts["iq"],
         parts["ik"], parts["iw"], pad, parts["memq"], parts["sbg"], parts["dsag"], parts["memg"]], axis=1)
    assert packed.shape[1] == PACKED_COLS
    return packed.astype(bf16)


def _proj_kernel(x_ref, g_ref, w_ref, wuk_ref, kvg_ref,
                 sbq_ref, sbk_ref, sbvt_ref, qlat_ref, ckv_ref, ckvt_ref, iq_ref, idxk_ref, iwt_ref,
                 memq_ref, gate_ref, *, T):
    tm = x_ref.shape[0]
    x = x_ref[...]
    ms = jnp.mean(x * x, axis=-1, keepdims=True)
    h = (x * lax.rsqrt(ms + RMS_EPS) * g_ref[...]).astype(bf16)

    def seg(off, n):
        return jnp.dot(h, w_ref[:, off:off + n], preferred_element_type=f32)

    a = seg(OFF_SBQ, 2 * SB_W)
    for hd in range(SB_HEADS):
        lo = hd * HEAD_DIM
        sbq_ref[hd] = (a[:, lo:lo + HEAD_DIM] * ATTN_SCALE).astype(bf16)
        sbk_ref[hd] = a[:, SB_W + lo:SB_W + lo + HEAD_DIM].astype(bf16)

    vt = seg(OFF_SBV, SB_W).T
    for r in range(tm // T):
        for hd in range(SB_HEADS):
            sbvt_ref[r, hd] = vt[hd * HEAD_DIM:(hd + 1) * HEAD_DIM, r * T:(r + 1) * T].astype(bf16)

    dq = seg(OFF_DSAQ, DSA_W)
    for hd in range(DSA_HEADS):
        q = dq[:, hd * HEAD_DIM:(hd + 1) * HEAD_DIM].astype(bf16)
        ql = jnp.dot(q, wuk_ref[hd], preferred_element_type=f32)
        qlat_ref[hd] = (ql * ATTN_SCALE).astype(bf16)

    c = seg(OFF_CKV, KV_RANK)
    cms = jnp.mean(c * c, axis=-1, keepdims=True)
    cn = c * lax.rsqrt(cms + RMS_EPS) * kvg_ref[...]
    ckv_ref[...] = cn.astype(bf16)
    cnt = cn.T
    for r in range(tm // T):
        ckvt_ref[r, :KV_RANK, :] = cnt[:, r * T:(r + 1) * T].astype(bf16)
        ckvt_ref[r, KV_RANK:, :] = jnp.ones((BF16_ROWS, T), bf16)

    e = seg(OFF_IQ, IDX_HEADS * IDX_DIM)
    for hd in range(IDX_HEADS):
        iq_ref[hd] = e[:, hd * IDX_DIM:(hd + 1) * IDX_DIM].astype(bf16)

    kw = seg(OFF_IKW, LANES)
    idxk_ref[...] = kw[:, :IDX_DIM].astype(bf16)
    iwt_ref[...] = kw.T[IDX_DIM:IDX_DIM + IDX_HEADS, :] * IDX_SCALE

    mq = seg(OFF_MEMQ, MEM_W)
    for hd in range(MEM_HEADS):
        memq_ref[hd] = (mq[:, hd * HEAD_DIM:(hd + 1) * HEAD_DIM] * ATTN_SCALE).astype(bf16)

    gt = seg(OFF_GATE, MIX_W)
    gate_ref[...] = gt * jax.nn.sigmoid(gt)


def _project(x, pre_g, w_packed, wuk_t, kv_g, tm, T):
    B, S, D = x.shape
    grid = (B, S // tm)
    rpt = tm // T
    row = lambda b, i: (b, i, 0)
    head = lambda b, i: (b, 0, i, 0)
    const2 = lambda b, i: (0, 0)
    const3 = lambda b, i: (0, 0, 0)
    out_shape = (
        jax.ShapeDtypeStruct((B, SB_HEADS, S, HEAD_DIM), bf16),
        jax.ShapeDtypeStruct((B, SB_HEADS, S, HEAD_DIM), bf16),
        jax.ShapeDtypeStruct((B, S // T, SB_HEADS, HEAD_DIM, T), bf16),
        jax.ShapeDtypeStruct((B, DSA_HEADS, S, KV_RANK), bf16),
        jax.ShapeDtypeStruct((B, S, KV_RANK), bf16),
        jax.ShapeDtypeStruct((B, S // T, KV_AUG, T), bf16),
        jax.ShapeDtypeStruct((B, IDX_HEADS, S, IDX_DIM), bf16),
        jax.ShapeDtypeStruct((B, S, IDX_DIM), bf16),
        jax.ShapeDtypeStruct((B, IDX_HEADS, S), f32),
        jax.ShapeDtypeStruct((B, MEM_HEADS, S, HEAD_DIM), bf16),
        jax.ShapeDtypeStruct((B, S, MIX_W), f32),
    )
    out_specs = (
        pl.BlockSpec((None, SB_HEADS, tm, HEAD_DIM), head),
        pl.BlockSpec((None, SB_HEADS, tm, HEAD_DIM), head),
        pl.BlockSpec((None, rpt, SB_HEADS, HEAD_DIM, T), lambda b, i: (b, i, 0, 0, 0)),
        pl.BlockSpec((None, DSA_HEADS, tm, KV_RANK), head),
        pl.BlockSpec((None, tm, KV_RANK), row),
        pl.BlockSpec((None, rpt, KV_AUG, T), lambda b, i: (b, i, 0, 0)),
        pl.BlockSpec((None, IDX_HEADS, tm, IDX_DIM), head),
        pl.BlockSpec((None, tm, IDX_DIM), row),
        pl.BlockSpec((None, IDX_HEADS, tm), lambda b, i: (b, 0, i)),
        pl.BlockSpec((None, MEM_HEADS, tm, HEAD_DIM), head),
        pl.BlockSpec((None, tm, MIX_W), row),
    )
    in_specs = [
        pl.BlockSpec((None, tm, D), row),
        pl.BlockSpec((1, D), const2),
        pl.BlockSpec((D, PACKED_COLS), const2),
        pl.BlockSpec((DSA_HEADS, HEAD_DIM, KV_RANK), const3),
        pl.BlockSpec((1, KV_RANK), const2),
    ]
    return pl.pallas_call(
        functools.partial(_proj_kernel, T=T),
        grid=grid, in_specs=in_specs, out_specs=out_specs, out_shape=out_shape,
        name="in_proj",
        compiler_params=pltpu.CompilerParams(
            dimension_semantics=("arbitrary", "arbitrary"), vmem_limit_bytes=VMEM_LIMIT_BYTES),
    )(x, pre_g.reshape(1, D), w_packed, wuk_t, kv_g.reshape(1, KV_RANK))


def _memkv_kernel(mem_ref, w_ref, k_ref, vt_ref):
    m = mem_ref[...].astype(bf16)
    kv = jnp.dot(m, w_ref[...], preferred_element_type=f32)
    vt = kv[:, MEM_W:].T
    for hd in range(MEM_HEADS):
        lo = hd * HEAD_DIM
        k_ref[hd] = kv[:, lo:lo + HEAD_DIM].astype(bf16)
        vt_ref[hd] = vt[lo:lo + HEAD_DIM, :].astype(bf16)


def _mem_kv(mem, w_mem_kv_bf):
    B, M, D = mem.shape
    out_shape = (jax.ShapeDtypeStruct((B, MEM_HEADS, M, HEAD_DIM), bf16),
                 jax.ShapeDtypeStruct((B, MEM_HEADS, HEAD_DIM, M), bf16))
    return pl.pallas_call(
        _memkv_kernel, grid=(B,),
        in_specs=[pl.BlockSpec((None, M, D), lambda b: (b, 0, 0)),
                  pl.BlockSpec((D, 2 * MEM_W), lambda b: (0, 0))],
        out_specs=(pl.BlockSpec((None, MEM_HEADS, M, HEAD_DIM), lambda b: (b, 0, 0, 0)),
                   pl.BlockSpec((None, MEM_HEADS, HEAD_DIM, M), lambda b: (b, 0, 0, 0))),
        out_shape=out_shape, name="mem_kv",
        compiler_params=pltpu.CompilerParams(
            dimension_semantics=("arbitrary",), vmem_limit_bytes=VMEM_LIMIT_BYTES),
    )(mem, w_mem_kv_bf)


def _t5_bucket(n):
    max_exact = N_BUCKETS // 2
    nf = jnp.maximum(n, 1).astype(f32)
    large = max_exact + (jnp.log(nf / max_exact) / math.log(MAX_DISTANCE / max_exact)
                         * (N_BUCKETS - max_exact)).astype(jnp.int32)
    large = jnp.minimum(large, N_BUCKETS - 1)
    return jnp.where(n < max_exact, n, large)


def _fold_keys(v, op):
    t = v.shape[0]
    v3 = v.reshape(t // SUBLANES, SUBLANES, v.shape[1])
    return op(v3, axis=0)


def _fold_rows16(m):
    parts = [m[i * BF16_ROWS:(i + 1) * BF16_ROWS] for i in range(m.shape[0] // BF16_ROWS)]
    while len(parts) > 1:
        parts = [parts[i] + parts[i + 1] for i in range(0, len(parts), 2)]
    return parts[0]


def _attn_kernel(relb_ref, x_ref, sbq_ref, sbk_ref, sbvt_ref, qlat_ref, ckv_ref, ckvt_ref, iq_ref, idxk_ref,
                 iwt_ref, memq_ref, memk_ref, memvt_ref, gate_ref, wuvt_ref, wout_ref, postg_ref,
                 out_ref,
                 keys_ref, k16_ref, lo16_ref, lg_ref, bias_ref, tri2_ref, tri_lt_ref, acc_ref,
                 z_ref, lb_ref, pl_ref, cum_ref, w_ref, sbacc_ref, mixt_ref,
                 *, T, topk):
    b = pl.program_id(0)
    qi = pl.program_id(1)
    nchunk = qi + 1
    i16 = jnp.int16
    key_l = lax.broadcasted_iota(jnp.int32, (T, T), 0)
    qry_l = lax.broadcasted_iota(jnp.int32, (T, T), 1)

    @pl.when((b == 0) & (qi == 0))
    def _init():
        later = jnp.where(qry_l > key_l, 1.0, 0.0).astype(bf16)
        tri2_ref[:, :T] = later
        tri2_ref[:, T:] = later
        tri_lt_ref[...] = jnp.where(qry_l < key_l, 1.0, 0.0).astype(bf16)
        for kind in range(3):
            dist = jnp.maximum(kind * T + qry_l - key_l, 0)
            bucket = _t5_bucket(dist)
            for hd in range(DSA_HEADS):
                tile = jnp.zeros((T, T), f32)
                for k in range(N_BUCKETS):
                    tile = jnp.where(bucket == k, relb_ref[k, hd], tile)
                bias_ref[kind, hd] = tile

    iq2d = iq_ref[...].reshape(IDX_HEADS * T, IDX_DIM)
    iwt = iwt_ref[...]

    def score_body(j, carry):
        ks = pl.multiple_of(j * T, T)
        dots = lax.dot_general(idxk_ref[pl.ds(ks, T), :], iq2d, NT, preferred_element_type=f32)
        score = jnp.zeros((T, T), f32)
        for hd in range(IDX_HEADS):
            score = score + iwt[hd:hd + 1, :] * jnp.maximum(dots[:, hd * T:(hd + 1) * T], 0.0)
        bits = pltpu.bitcast(score, jnp.int32)
        key = jnp.where(bits < 0, INT_MIN - bits, bits)
        valid = (ks + key_l) <= (qi * T + qry_l)
        key = jnp.where(valid, key, INT_MIN)
        keys_ref[j] = key
        k16_ref[j] = (key >> 16).astype(i16)
        return carry

    lax.fori_loop(0, nchunk, score_body, 0)

    def count16(ref, cand, strict=False):
        c16 = cand.astype(i16)

        def body(j, acc):
            k = ref[j]
            hit = (k > c16) if strict else (k >= c16)
            return acc + _fold_rows16(jnp.where(hit, i16(1), i16(0)))

        acc = lax.fori_loop(0, nchunk, body, jnp.zeros((BF16_ROWS, T), i16))
        return jnp.sum(acc.astype(f32), axis=0, keepdims=True)

    kf = float(topk)
    half = 1 << 15
    n_all = (nchunk * T).astype(f32)
    n_pos = count16(k16_ref, jnp.zeros((1, T), jnp.int32))
    hi0 = jnp.where(n_pos >= kf, 0, -half).astype(jnp.int32)
    n0 = jnp.where(n_pos >= kf, n_pos, n_all)

    def hi_body(i, carry):
        hi, n_hi = carry
        cand = hi | (jnp.int32(1) << (14 - i))
        n = count16(k16_ref, cand)
        ok = n >= kf
        return jnp.where(ok, cand, hi), jnp.where(ok, n, n_hi)

    thr_hi, n_hi = lax.fori_loop(0, 15, hi_body, (hi0, n0))
    n_above = count16(k16_ref, thr_hi, strict=True)

    def lo_fill(j, carry):
        key = keys_ref[j]
        low = (key & 0xFFFF) - half
        lo16_ref[j] = jnp.where((key >> 16) == thr_hi, low, -half).astype(i16)
        return carry

    lax.fori_loop(0, nchunk, lo_fill, 0)

    def lo_body(i, carry):
        lo, n_lo = carry
        cand = lo | (jnp.int32(1) << (15 - i))
        n = n_above + count16(lo16_ref, cand - half)
        ok = n >= kf
        return jnp.where(ok, cand, lo), jnp.where(ok, n, n_lo)

    thr_lo, n_ge = lax.fori_loop(0, 16, lo_body, (jnp.zeros((1, T), jnp.int32), n_hi))
    thr = thr_hi * (1 << 16) + thr_lo
    has_ties = jnp.max(jnp.where((n_ge > kf) & (thr > INT_MIN), 1.0, 0.0)) > 0.0
    thr_floor = jnp.maximum(thr, INT_MIN + 1)

    def count_above_thr(_):
        def body(j, acc):
            return acc + _fold_keys(jnp.where(keys_ref[j] > thr, 1.0, 0.0), jnp.sum)
        acc = lax.fori_loop(0, nchunk, body, jnp.zeros((SUBLANES, T), f32))
        return jnp.sum(acc, axis=0, keepdims=True)

    need = kf - lax.cond(has_ties, count_above_thr, lambda _: jnp.zeros((1, T), f32), 0)

    def dsa_logits(j, carry):
        eq_seen, maxes = carry
        ks = pl.multiple_of(j * T, T)
        k = keys_ref[j]

        def tie_cut(_):
            eq = jnp.where(k == thr, 1.0, 0.0)
            prefix = jnp.dot(tri_lt_ref[...], eq.astype(bf16), preferred_element_type=f32) + eq_seen
            bump = jnp.where(prefix >= need, 1, 0).astype(jnp.int32)
            return (jnp.maximum(thr + bump, INT_MIN + 1),
                    eq_seen + jnp.sum(_fold_keys(eq, jnp.sum), axis=0, keepdims=True))

        def plain_cut(_):
            return jnp.broadcast_to(thr_floor, (T, T)), eq_seen

        cut, eq_seen = lax.cond(has_ties, tie_cut, plain_cut, 0)
        sel = k >= cut
        ckv = ckv_ref[pl.ds(ks, T), :]
        kind = jnp.minimum(qi - j, 2)

        def head_logits(hd):
            return lax.dot_general(ckv, qlat_ref[hd], NT, preferred_element_type=f32)

        new_maxes = []
        ahead = head_logits(0)
        for hd in range(DSA_HEADS):
            logits = ahead
            if hd + 1 < DSA_HEADS:
                ahead = head_logits(hd + 1)
            lg = jnp.where(sel, logits + bias_ref[kind, hd], NEG_BIG)
            lg_ref[j, :, hd * T:(hd + 1) * T] = lg
            new_maxes.append(jnp.maximum(maxes[hd], _fold_keys(lg, jnp.max)))
        return eq_seen, tuple(new_maxes)

    init_max = tuple(jnp.full((SUBLANES, T), NEG_BIG, f32) for _ in range(DSA_HEADS))
    _, maxes = lax.fori_loop(0, nchunk, dsa_logits, (jnp.zeros((1, T), f32), init_max))
    row_max = [jnp.max(m, axis=0, keepdims=True) for m in maxes]

    acc_ref[...] = jnp.zeros(acc_ref.shape, f32)

    def dsa_values(j, carry):
        ckvt = ckvt_ref[j]
        for hd in range(DSA_HEADS):
            p = jnp.exp(lg_ref[j, :, hd * T:(hd + 1) * T] - row_max[hd]).astype(bf16)
            acc_ref[hd] += jnp.dot(ckvt, p, preferred_element_type=f32)
        return carry

    lax.fori_loop(0, nchunk, dsa_values, 0)

    for hd in range(DSA_HEADS):
        a = acc_ref[hd]
        o_lat = (a[:KV_RANK] / a[KV_RANK:KV_RANK + 1]).astype(bf16)
        mixt_ref[SB_W + hd * HEAD_DIM:SB_W + (hd + 1) * HEAD_DIM, :] = jnp.dot(
            wuvt_ref[hd], o_lat, preferred_element_type=f32)

    sbacc_ref[...] = jnp.zeros(sbacc_ref.shape, f32)
    RB = 64
    sign_bit = jnp.int32(INT_MIN)

    def causal_rows(r0):
        return (lax.broadcasted_iota(jnp.int32, (RB, T), 0) + r0) < lax.broadcasted_iota(jnp.int32, (RB, T), 1)

    def sb_scores(j, hd):
        ks = pl.multiple_of(j * T, T)
        z_ref[hd % 2] = lax.dot_general(sbk_ref[hd, pl.ds(ks, T), :], sbq_ref[hd], NT,
                                        preferred_element_type=f32)

    def sb_logs(hd, masked):
        s = hd % 2
        first = None
        for r0 in range(0, T, RB):
            z = z_ref[s, r0:r0 + RB, :]
            neg_abs = pltpu.bitcast(pltpu.bitcast(z, jnp.int32) | sign_bit, f32)
            p = jnp.maximum(z, 0.0) + jnp.log(1.0 + jnp.exp(neg_abs))
            if masked:
                p = jnp.where(causal_rows(r0), p, 0.0)
            lb_ref[s, r0:r0 + RB, :] = z - p
            hi = p.astype(bf16)
            pl_ref[s, r0:r0 + RB, :] = hi
            pl_ref[s, T + r0:T + r0 + RB, :] = (p - hi.astype(f32)).astype(bf16)
            if r0 == 0:
                first = p[0:1, :]
        return first

    def sb_suffix(hd):
        s = hd % 2
        cum_ref[s] = jnp.dot(tri2_ref[...], pl_ref[s], preferred_element_type=f32)

    def sb_weights(hd, carry, first, masked):
        s = hd % 2
        for r0 in range(0, T, RB):
            w = jnp.exp(lb_ref[s, r0:r0 + RB, :] - cum_ref[s, r0:r0 + RB, :] + carry)
            if masked:
                w = jnp.where(causal_rows(r0), w, 0.0)
            w_ref[s, r0:r0 + RB, :] = w.astype(bf16)
        return carry - (cum_ref[s, 0:1, :] + first)

    def sb_values(j, hd):
        sbacc_ref[hd] += jnp.dot(sbvt_ref[j, hd], w_ref[hd % 2], preferred_element_type=f32)

    def sb_chunk(j, carries, masked):
        new = [None] * SB_HEADS
        firsts = [None] * SB_HEADS
        sb_scores(j, 0)
        for hd in range(SB_HEADS):
            if hd + 1 < SB_HEADS:
                sb_scores(j, hd + 1)
            firsts[hd] = sb_logs(hd, masked)
            sb_suffix(hd)
            if hd >= 1:
                new[hd - 1] = sb_weights(hd - 1, carries[hd - 1], firsts[hd - 1], masked)
                sb_values(j, hd - 1)
        last = SB_HEADS - 1
        new[last] = sb_weights(last, carries[last], firsts[last], masked)
        sb_values(j, last)
        return tuple(new)

    carries = sb_chunk(qi, tuple(jnp.zeros((1, T), f32) for _ in range(SB_HEADS)), True)

    def carry_max(cs):
        return jnp.max(functools.reduce(jnp.maximum, cs))

    def sb_cond(state):
        i, cmax, _ = state
        return (i < nchunk) & (cmax >= EXP_ZERO_BELOW)

    def sb_body(state):
        i, _, cs = state
        cs = sb_chunk(qi - i, cs, False)
        return i + 1, carry_max(cs), cs

    lax.while_loop(sb_cond, sb_body, (jnp.int32(1), carry_max(carries), carries))
    for hd in range(SB_HEADS):
        mixt_ref[hd * HEAD_DIM:(hd + 1) * HEAD_DIM, :] = sbacc_ref[hd]

    for hd in range(MEM_HEADS):
        lg = lax.dot_general(memk_ref[hd], memq_ref[hd], NT, preferred_element_type=f32)
        e = jnp.exp(lg - jnp.max(lg, axis=0, keepdims=True))
        p = e / jnp.sum(e, axis=0, keepdims=True)
        lo = SB_W + DSA_W + hd * HEAD_DIM
        mixt_ref[lo:lo + HEAD_DIM, :] = jnp.dot(memvt_ref[hd], p.astype(bf16), preferred_element_type=f32)

    gated = (mixt_ref[...].T * gate_ref[...]).astype(bf16)
    y = jnp.dot(gated, wout_ref[...], preferred_element_type=f32)
    ms = jnp.mean(y * y, axis=-1, keepdims=True)
    out_ref[...] = x_ref[...] + y * lax.rsqrt(ms + RMS_EPS) * postg_ref[...]


def _attention(x, proj, memk, memvt, wuvt_h, wout_bf, post_g, rel_bias, T, topk):
    sbq, sbk, sbvt, qlat, ckv, ckvt, iq, idxk, iwt, memq, gate = proj
    B, S, D = x.shape
    nq = S // T
    grid = (B, nq)
    qrow = lambda b, i: (b, i, 0)
    qhead = lambda b, i: (b, 0, i, 0)
    krow = lambda b, i: (b, 0, 0)
    khead = lambda b, i: (b, 0, 0, 0)
    const2 = lambda b, i: (0, 0)
    const3 = lambda b, i: (0, 0, 0)
    in_specs = [
        pl.BlockSpec(memory_space=pltpu.SMEM),
        pl.BlockSpec((None, T, D), qrow),
        pl.BlockSpec((None, SB_HEADS, T, HEAD_DIM), qhead),
        pl.BlockSpec((None, SB_HEADS, S, HEAD_DIM), khead),
        pl.BlockSpec((None, nq, SB_HEADS, HEAD_DIM, T), lambda b, i: (b, 0, 0, 0, 0)),
        pl.BlockSpec((None, DSA_HEADS, T, KV_RANK), qhead),
        pl.BlockSpec((None, S, KV_RANK), krow),
        pl.BlockSpec((None, nq, KV_AUG, T), khead),
        pl.BlockSpec((None, IDX_HEADS, T, IDX_DIM), qhead),
        pl.BlockSpec((None, S, IDX_DIM), krow),
        pl.BlockSpec((None, IDX_HEADS, T), lambda b, i: (b, 0, i)),
        pl.BlockSpec((None, MEM_HEADS, T, HEAD_DIM), qhead),
        pl.BlockSpec((None, MEM_HEADS, N_MEM, HEAD_DIM), khead),
        pl.BlockSpec((None, MEM_HEADS, HEAD_DIM, N_MEM), khead),
        pl.BlockSpec((None, T, MIX_W), qrow),
        pl.BlockSpec((DSA_HEADS, HEAD_DIM, KV_RANK), const3),
        pl.BlockSpec((MIX_W, D), const2),
        pl.BlockSpec((1, D), const2),
    ]
    scratch = [
        pltpu.VMEM((nq, T, T), jnp.int32),
        pltpu.VMEM((nq, T, T), jnp.int16),
        pltpu.VMEM((nq, T, T), jnp.int16),
        pltpu.VMEM((nq, T, DSA_HEADS * T), f32),
        pltpu.VMEM((3, DSA_HEADS, T, T), f32),
        pltpu.VMEM((T, 2 * T), bf16),
        pltpu.VMEM((T, T), bf16),
        pltpu.VMEM((DSA_HEADS, KV_AUG, T), f32),
        pltpu.VMEM((2, T, T), f32),
        pltpu.VMEM((2, T, T), f32),
        pltpu.VMEM((2, 2 * T, T), bf16),
        pltpu.VMEM((2, T, T), f32),
        pltpu.VMEM((2, T, T), bf16),
        pltpu.VMEM((SB_HEADS, HEAD_DIM, T), f32),
        pltpu.VMEM((MIX_W, T), f32),
    ]
    return pl.pallas_call(
        functools.partial(_attn_kernel, T=T, topk=topk),
        grid=grid, in_specs=in_specs,
        out_specs=pl.BlockSpec((None, T, D), qrow),
        out_shape=jax.ShapeDtypeStruct((B, S, D), f32),
        scratch_shapes=scratch, name="hybrid_attn",
        compiler_params=pltpu.CompilerParams(
            dimension_semantics=("arbitrary", "arbitrary"), vmem_limit_bytes=VMEM_LIMIT_BYTES),
    )(rel_bias, x, sbq, sbk, sbvt, qlat, ckv, ckvt, iq, idxk, iwt, memq, memk, memvt, gate,
      wuvt_h, wout_bf, post_g.reshape(1, D))


def _tile_sizes(S):
    T = 256
    assert S % T == 0 and T >= MAX_DISTANCE
    tm = 512 if S % 512 == 0 else T
    return tm, T


def kernel(x, mem, pre_norm_g, post_norm_g, w_in, w_uk, w_uv, kv_norm_g, w_mem_kv, w_out, rel_bias):
    B, S, D = x.shape
    assert D == D_MODEL and mem.shape[1] == N_MEM
    topk = min(TOPK_MAX, S // 4)
    tm, T = _tile_sizes(S)
    for layer in range(w_in.shape[0]):
        w_packed = _pack_w_in(w_in[layer])
        wuk_t = jnp.transpose(w_uk[layer], (1, 2, 0)).astype(bf16)
        wuvt_h = jnp.transpose(w_uv[layer], (1, 2, 0)).astype(bf16)
        proj = _project(x, pre_norm_g[layer], w_packed, wuk_t, kv_norm_g[layer], tm, T)
        memk, memvt = _mem_kv(mem, w_mem_kv[layer].astype(bf16))
        x = _attention(x, proj, memk, memvt, wuvt_h, w_out[layer].astype(bf16), post_norm_g[layer],
                       rel_bias, T, topk)
    return x
```

```python
import functools
import math

import jax
import jax.numpy as jnp
from jax import lax
from jax.experimental import pallas as pl
from jax.experimental.pallas import tpu as pltpu

D_MODEL = 1024
N_MEM = 256
HEAD_DIM = 64
SB_HEADS = 6
DSA_HEADS = 6
MEM_HEADS = 4
SB_W = SB_HEADS * HEAD_DIM
DSA_W = DSA_HEADS * HEAD_DIM
MEM_W = MEM_HEADS * HEAD_DIM
MIX_W = SB_W + DSA_W + MEM_W
KV_RANK = 128
IDX_HEADS = 8
IDX_DIM = 32
TOPK_MAX = 256
N_BUCKETS = 32
MAX_DISTANCE = 128
RMS_EPS = 1e-6

LANES = 128
SUBLANES = 8
BF16_ROWS = 16
ATTN_SCALE = HEAD_DIM ** -0.5
IDX_SCALE = (IDX_HEADS * IDX_DIM) ** -0.5
INT_MIN = -2 ** 31
NEG_BIG = -1e30
EXP_ZERO_BELOW = -105.0
VMEM_LIMIT_BYTES = 56 * 1024 * 1024
KV_AUG = KV_RANK + BF16_ROWS

OFF_SBQ = 0
OFF_SBK = OFF_SBQ + SB_W
OFF_SBV = OFF_SBK + SB_W
OFF_DSAQ = OFF_SBV + SB_W
OFF_CKV = OFF_DSAQ + DSA_W
OFF_IQ = OFF_CKV + KV_RANK
OFF_IKW = OFF_IQ + IDX_HEADS * IDX_DIM
OFF_MEMQ = OFF_IKW + LANES
OFF_GATE = OFF_MEMQ + MEM_W
PACKED_COLS = OFF_GATE + MIX_W

f32 = jnp.float32
bf16 = jnp.bfloat16
NT = (((1,), (1,)), ((), ()))


def _pack_w_in(w):
    o = 0
    parts = {}
    for name, n in (("sbq", SB_W), ("sbk", SB_W), ("sbv", SB_W), ("sbg", SB_W), ("dsaq", DSA_W),
                    ("ckv", KV_RANK), ("dsag", DSA_W), ("iq", IDX_HEADS * IDX_DIM), ("ik", IDX_DIM),
                    ("iw", IDX_HEADS), ("memq", MEM_W), ("memg", MEM_W)):
        parts[name] = w[:, o:o + n]
        o += n
    pad = jnp.zeros((w.shape[0], LANES - IDX_DIM - IDX_HEADS), w.dtype)
    packed = jnp.concatenate(
        [parts["sbq"], parts["sbk"], parts["sbv"], parts["dsaq"], parts["ckv"], parts["iq"],
         parts["ik"], parts["iw"], pad, parts["memq"], parts["sbg"], parts["dsag"], parts["memg"]], axis=1)
    assert packed.shape[1] == PACKED_COLS
    return packed.astype(bf16)


def _proj_kernel(x_ref, g_ref, w_ref, wuk_ref, kvg_ref,
                 sbq_ref, sbk_ref, sbvt_ref, qlat_ref, ckv_ref, ckvt_ref, iq_ref, idxk_ref, iwt_ref,
                 memq_ref, gate_ref, *, T):
    tm = x_ref.shape[0]
    x = x_ref[...]
    ms = jnp.mean(x * x, axis=-1, keepdims=True)
    h = (x * lax.rsqrt(ms + RMS_EPS) * g_ref[...]).astype(bf16)

    def seg(off, n):
        return jnp.dot(h, w_ref[:, off:off + n], preferred_element_type=f32)

    a = seg(OFF_SBQ, 2 * SB_W)
    for hd in range(SB_HEADS):
        lo = hd * HEAD_DIM
        sbq_ref[hd] = (a[:, lo:lo + HEAD_DIM] * ATTN_SCALE).astype(bf16)
        sbk_ref[hd] = a[:, SB_W + lo:SB_W + lo + HEAD_DIM].astype(bf16)

    vt = seg(OFF_SBV, SB_W).T
    for r in range(tm // T):
        for hd in range(SB_HEADS):
            sbvt_ref[r, hd] = vt[hd * HEAD_DIM:(hd + 1) * HEAD_DIM, r * T:(r + 1) * T].astype(bf16)

    dq = seg(OFF_DSAQ, DSA_W)
    for hd in range(DSA_HEADS):
        q = dq[:, hd * HEAD_DIM:(hd + 1) * HEAD_DIM].astype(bf16)
        ql = jnp.dot(q, wuk_ref[hd], preferred_element_type=f32)
        qlat_ref[hd] = (ql * ATTN_SCALE).astype(bf16)

    c = seg(OFF_CKV, KV_RANK)
    cms = jnp.mean(c * c, axis=-1, keepdims=True)
    cn = c * lax.rsqrt(cms + RMS_EPS) * kvg_ref[...]
    ckv_ref[...] = cn.astype(bf16)
    cnt = cn.T
    for r in range(tm // T):
        ckvt_ref[r, :KV_RANK, :] = cnt[:, r * T:(r + 1) * T].astype(bf16)
        ckvt_ref[r, KV_RANK:, :] = jnp.ones((BF16_ROWS, T), bf16)

    e = seg(OFF_IQ, IDX_HEADS * IDX_DIM)
    for hd in range(IDX_HEADS):
        iq_ref[hd] = e[:, hd * IDX_DIM:(hd + 1) * IDX_DIM].astype(bf16)

    kw = seg(OFF_IKW, LANES)
    idxk_ref[...] = kw[:, :IDX_DIM].astype(bf16)
    iwt_ref[...] = kw.T[IDX_DIM:IDX_DIM + IDX_HEADS, :] * IDX_SCALE

    mq = seg(OFF_MEMQ, MEM_W)
    for hd in range(MEM_HEADS):
        memq_ref[hd] = (mq[:, hd * HEAD_DIM:(hd + 1) * HEAD_DIM] * ATTN_SCALE).astype(bf16)

    gt = seg(OFF_GATE, MIX_W)
    gate_ref[...] = gt * jax.nn.sigmoid(gt)


def _project(x, pre_g, w_packed, wuk_t, kv_g, tm, T):
    B, S, D = x.shape
    grid = (B, S // tm)
    rpt = tm // T
    row = lambda b, i: (b, i, 0)
    head = lambda b, i: (b, 0, i, 0)
    const2 = lambda b, i: (0, 0)
    const3 = lambda b, i: (0, 0, 0)
    out_shape = (
        jax.ShapeDtypeStruct((B, SB_HEADS, S, HEAD_DIM), bf16),
        jax.ShapeDtypeStruct((B, SB_HEADS, S, HEAD_DIM), bf16),
        jax.ShapeDtypeStruct((B, S // T, SB_HEADS, HEAD_DIM, T), bf16),
        jax.ShapeDtypeStruct((B, DSA_HEADS, S, KV_RANK), bf16),
        jax.ShapeDtypeStruct((B, S, KV_RANK), bf16),
        jax.ShapeDtypeStruct((B, S // T, KV_AUG, T), bf16),
        jax.ShapeDtypeStruct((B, IDX_HEADS, S, IDX_DIM), bf16),
        jax.ShapeDtypeStruct((B, S, IDX_DIM), bf16),
        jax.ShapeDtypeStruct((B, IDX_HEADS, S), f32),
        jax.ShapeDtypeStruct((B, MEM_HEADS, S, HEAD_DIM), bf16),
        jax.ShapeDtypeStruct((B, S, MIX_W), f32),
    )
    out_specs = (
        pl.BlockSpec((None, SB_HEADS, tm, HEAD_DIM), head),
        pl.BlockSpec((None, SB_HEADS, tm, HEAD_DIM), head),
        pl.BlockSpec((None, rpt, SB_HEADS, HEAD_DIM, T), lambda b, i: (b, i, 0, 0, 0)),
        pl.BlockSpec((None, DSA_HEADS, tm, KV_RANK), head),
        pl.BlockSpec((None, tm, KV_RANK), row),
        pl.BlockSpec((None, rpt, KV_AUG, T), lambda b, i: (b, i, 0, 0)),
        pl.BlockSpec((None, IDX_HEADS, tm, IDX_DIM), head),
        pl.BlockSpec((None, tm, IDX_DIM), row),
        pl.BlockSpec((None, IDX_HEADS, tm), lambda b, i: (b, 0, i)),
        pl.BlockSpec((None, MEM_HEADS, tm, HEAD_DIM), head),
        pl.BlockSpec((None, tm, MIX_W), row),
    )
    in_specs = [
        pl.BlockSpec((None, tm, D), row),
        pl.BlockSpec((1, D), const2),
        pl.BlockSpec((D, PACKED_COLS), const2),
        pl.BlockSpec((DSA_HEADS, HEAD_DIM, KV_RANK), const3),
        pl.BlockSpec((1, KV_RANK), const2),
    ]
    return pl.pallas_call(
        functools.partial(_proj_kernel, T=T),
        grid=grid, in_specs=in_specs, out_specs=out_specs, out_shape=out_shape,
        name="in_proj",
        compiler_params=pltpu.CompilerParams(
            dimension_semantics=("arbitrary", "arbitrary"), vmem_limit_bytes=VMEM_LIMIT_BYTES),
    )(x, pre_g.reshape(1, D), w_packed, wuk_t, kv_g.reshape(1, KV_RANK))


def _memkv_kernel(mem_ref, w_ref, k_ref, vt_ref):
    m = mem_ref[...].astype(bf16)
    kv = jnp.dot(m, w_ref[...], preferred_element_type=f32)
    vt = kv[:, MEM_W:].T
    for hd in range(MEM_HEADS):
        lo = hd * HEAD_DIM
        k_ref[hd] = kv[:, lo:lo + HEAD_DIM].astype(bf16)
        vt_ref[hd] = vt[lo:lo + HEAD_DIM, :].astype(bf16)


def _mem_kv(mem, w_mem_kv_bf):
    B, M, D = mem.shape
    out_shape = (jax.ShapeDtypeStruct((B, MEM_HEADS, M, HEAD_DIM), bf16),
                 jax.ShapeDtypeStruct((B, MEM_HEADS, HEAD_DIM, M), bf16))
    return pl.pallas_call(
        _memkv_kernel, grid=(B,),
        in_specs=[pl.BlockSpec((None, M, D), lambda b: (b, 0, 0)),
                  pl.BlockSpec((D, 2 * MEM_W), lambda b: (0, 0))],
        out_specs=(pl.BlockSpec((None, MEM_HEADS, M, HEAD_DIM), lambda b: (b, 0, 0, 0)),
                   pl.BlockSpec((None, MEM_HEADS, HEAD_DIM, M), lambda b: (b, 0, 0, 0))),
        out_shape=out_shape, name="mem_kv",
        compiler_params=pltpu.CompilerParams(
            dimension_semantics=("arbitrary",), vmem_limit_bytes=VMEM_LIMIT_BYTES),
    )(mem, w_mem_kv_bf)


def _t5_bucket(n):
    max_exact = N_BUCKETS // 2
    nf = jnp.maximum(n, 1).astype(f32)
    large = max_exact + (jnp.log(nf / max_exact) / math.log(MAX_DISTANCE / max_exact)
                         * (N_BUCKETS - max_exact)).astype(jnp.int32)
    large = jnp.minimum(large, N_BUCKETS - 1)
    return jnp.where(n < max_exact, n, large)


def _bias_kernel(relb_ref, bias_ref, *, T):
    key_l = lax.broadcasted_iota(jnp.int32, (T, T), 0)
    qry_l = lax.broadcasted_iota(jnp.int32, (T, T), 1)
    for kind in range(3):
        bucket = _t5_bucket(jnp.maximum(kind * T + qry_l - key_l, 0))
        for hd in range(DSA_HEADS):
            tile = jnp.zeros((T, T), f32)
            for k in range(N_BUCKETS):
                tile = jnp.where(bucket == k, relb_ref[k, hd], tile)
            bias_ref[kind, hd] = tile


def _bias_tiles(rel_bias, T):
    return pl.pallas_call(
        functools.partial(_bias_kernel, T=T),
        in_specs=[pl.BlockSpec(memory_space=pltpu.SMEM)],
        out_shape=jax.ShapeDtypeStruct((3, DSA_HEADS, T, T), f32), name="t5_bias",
        compiler_params=pltpu.CompilerParams(vmem_limit_bytes=VMEM_LIMIT_BYTES),
    )(rel_bias)


def _fold_keys(v, op):
    t = v.shape[0]
    v3 = v.reshape(t // SUBLANES, SUBLANES, v.shape[1])
    return op(v3, axis=0)


def _loop(lo, hi, body, init):
    if isinstance(lo, int) and isinstance(hi, int):
        val = init
        for i in range(lo, hi):
            val = body(i, val)
        return val
    return lax.fori_loop(lo, hi, body, init)


def _fold_rows16(m):
    parts = [m[i * BF16_ROWS:(i + 1) * BF16_ROWS] for i in range(m.shape[0] // BF16_ROWS)]
    while len(parts) > 1:
        parts = [parts[i] + parts[i + 1] for i in range(0, len(parts), 2)]
    return parts[0]


def _attn_kernel(bias_ref, x_ref, sbq_ref, sbk_ref, sbvt_ref, qlat_ref, ckv_ref, ckvt_ref, iq_ref, idxk_ref,
                 iwt_ref, memq_ref, memk_ref, memvt_ref, gate_ref, wuvt_ref, wout_ref, postg_ref,
                 out_ref,
                 keys_ref, k16_ref, lo16_ref, lg_ref, tri2_ref, tri_lt_ref, acc_ref,
                 z_ref, lb_ref, pl_ref, cum_ref, w_ref, sbacc_ref, mixt_ref,
                 *, T, topk, qi):
    b = pl.program_id(0)
    nchunk = qi + 1
    i16 = jnp.int16
    key_l = lax.broadcasted_iota(jnp.int32, (T, T), 0)
    qry_l = lax.broadcasted_iota(jnp.int32, (T, T), 1)

    @pl.when(b == 0)
    def _init():
        later = jnp.where(qry_l > key_l, 1.0, 0.0).astype(bf16)
        tri2_ref[:, :T] = later
        tri2_ref[:, T:] = later
        tri_lt_ref[...] = jnp.where(qry_l < key_l, 1.0, 0.0).astype(bf16)

    iq2d = iq_ref[...].reshape(IDX_HEADS * T, IDX_DIM)
    iwt = iwt_ref[...]

    def score_body(j, carry):
        ks = j * T
        dots = lax.dot_general(idxk_ref[pl.ds(ks, T), :], iq2d, NT, preferred_element_type=f32)
        score = jnp.zeros((T, T), f32)
        for hd in range(IDX_HEADS):
            score = score + iwt[hd:hd + 1, :] * jnp.maximum(dots[:, hd * T:(hd + 1) * T], 0.0)
        bits = pltpu.bitcast(score, jnp.int32)
        key = jnp.where(bits < 0, INT_MIN - bits, bits)
        if j == qi:
            key = jnp.where(key_l <= qry_l, key, INT_MIN)
        keys_ref[j] = key
        k16_ref[j] = (key >> 16).astype(i16)
        return carry

    _loop(0, nchunk, score_body, 0)

    def count16(ref, cand, strict=False):
        c16 = cand.astype(i16)

        def body(j, acc):
            k = ref[j]
            hit = (k > c16) if strict else (k >= c16)
            return acc + _fold_rows16(jnp.where(hit, i16(1), i16(0)))

        acc = _loop(0, nchunk, body, jnp.zeros((BF16_ROWS, T), i16))
        return jnp.sum(acc.astype(f32), axis=0, keepdims=True)

    kf = float(topk)
    half = 1 << 15
    n_all = float(nchunk * T)
    n_pos = count16(k16_ref, jnp.zeros((1, T), jnp.int32))
    hi0 = jnp.where(n_pos >= kf, 0, -half).astype(jnp.int32)
    n0 = jnp.where(n_pos >= kf, n_pos, n_all)

    def hi_body(i, carry):
        hi, n_hi = carry
        cand = hi | (jnp.int32(1) << (14 - i))
        n = count16(k16_ref, cand)
        ok = n >= kf
        return jnp.where(ok, cand, hi), jnp.where(ok, n, n_hi)

    thr_hi, n_hi = lax.fori_loop(0, 15, hi_body, (hi0, n0))
    n_above = count16(k16_ref, thr_hi, strict=True)

    def lo_fill(j, carry):
        key = keys_ref[j]
        low = (key & 0xFFFF) - half
        lo16_ref[j] = jnp.where((key >> 16) == thr_hi, low, -half).astype(i16)
        return carry

    _loop(0, nchunk, lo_fill, 0)

    def lo_body(i, carry):
        lo, n_lo = carry
        cand = lo | (jnp.int32(1) << (15 - i))
        n = n_above + count16(lo16_ref, cand - half)
        ok = n >= kf
        return jnp.where(ok, cand, lo), jnp.where(ok, n, n_lo)

    thr_lo, n_ge = lax.fori_loop(0, 16, lo_body, (jnp.zeros((1, T), jnp.int32), n_hi))
    thr = thr_hi * (1 << 16) + thr_lo
    has_ties = jnp.max(jnp.where((n_ge > kf) & (thr > INT_MIN), 1.0, 0.0)) > 0.0
    thr_floor = jnp.maximum(thr, INT_MIN + 1)

    def count_above_thr(_):
        def body(j, acc):
            return acc + _fold_keys(jnp.where(keys_ref[j] > thr, 1.0, 0.0), jnp.sum)
        acc = _loop(0, nchunk, body, jnp.zeros((SUBLANES, T), f32))
        return jnp.sum(acc, axis=0, keepdims=True)

    need = kf - lax.cond(has_ties, count_above_thr, lambda _: jnp.zeros((1, T), f32), 0)

    def dsa_logits(j, carry):
        eq_seen, maxes = carry
        ks = j * T
        k = keys_ref[j]

        def tie_cut(_):
            eq = jnp.where(k == thr, 1.0, 0.0)
            prefix = jnp.dot(tri_lt_ref[...], eq.astype(bf16), preferred_element_type=f32) + eq_seen
            bump = jnp.where(prefix >= need, 1, 0).astype(jnp.int32)
            return (jnp.maximum(thr + bump, INT_MIN + 1),
                    eq_seen + jnp.sum(_fold_keys(eq, jnp.sum), axis=0, keepdims=True))

        def plain_cut(_):
            return jnp.broadcast_to(thr_floor, (T, T)), eq_seen

        cut, eq_seen = lax.cond(has_ties, tie_cut, plain_cut, 0)
        sel = k >= cut
        ckv = ckv_ref[pl.ds(ks, T), :]
        kind = min(qi - j, 2)

        def head_logits(hd):
            return lax.dot_general(ckv, qlat_ref[hd], NT, preferred_element_type=f32)

        new_maxes = []
        ahead = head_logits(0)
        for hd in range(DSA_HEADS):
            logits = ahead
            if hd + 1 < DSA_HEADS:
                ahead = head_logits(hd + 1)
            lg = jnp.where(sel, logits + bias_ref[kind, hd], NEG_BIG)
            lg_ref[j, :, hd * T:(hd + 1) * T] = lg
            new_maxes.append(jnp.maximum(maxes[hd], _fold_keys(lg, jnp.max)))
        return eq_seen, tuple(new_maxes)

    init_max = tuple(jnp.full((SUBLANES, T), NEG_BIG, f32) for _ in range(DSA_HEADS))
    _, maxes = _loop(0, nchunk, dsa_logits, (jnp.zeros((1, T), f32), init_max))
    row_max = [jnp.max(m, axis=0, keepdims=True) for m in maxes]

    acc_ref[...] = jnp.zeros(acc_ref.shape, f32)

    def dsa_values(j, carry):
        ckvt = ckvt_ref[j]
        for hd in range(DSA_HEADS):
            p = jnp.exp(lg_ref[j, :, hd * T:(hd + 1) * T] - row_max[hd]).astype(bf16)
            acc_ref[hd] += jnp.dot(ckvt, p, preferred_element_type=f32)
        return carry

    _loop(0, nchunk, dsa_values, 0)

    for hd in range(DSA_HEADS):
        a = acc_ref[hd]
        o_lat = (a[:KV_RANK] / a[KV_RANK:KV_RANK + 1]).astype(bf16)
        mixt_ref[SB_W + hd * HEAD_DIM:SB_W + (hd + 1) * HEAD_DIM, :] = jnp.dot(
            wuvt_ref[hd], o_lat, preferred_element_type=f32)

    sbacc_ref[...] = jnp.zeros(sbacc_ref.shape, f32)
    RB = 64
    sign_bit = jnp.int32(INT_MIN)

    def causal_rows(r0):
        return (lax.broadcasted_iota(jnp.int32, (RB, T), 0) + r0) < lax.broadcasted_iota(jnp.int32, (RB, T), 1)

    def sb_scores(j, hd, s):
        ks = j * T if isinstance(j, int) else pl.multiple_of(j * T, T)
        z_ref[s] = lax.dot_general(sbk_ref[hd, pl.ds(ks, T), :], sbq_ref[hd], NT, preferred_element_type=f32)

    def sb_logs(s, masked):
        first = None
        for r0 in range(0, T, RB):
            z = z_ref[s, r0:r0 + RB, :]
            neg_abs = pltpu.bitcast(pltpu.bitcast(z, jnp.int32) | sign_bit, f32)
            p = jnp.maximum(z, 0.0) + jnp.log(1.0 + jnp.exp(neg_abs))
            if masked:
                p = jnp.where(causal_rows(r0), p, 0.0)
            lb_ref[s, r0:r0 + RB, :] = z - p
            hi = p.astype(bf16)
            pl_ref[s, r0:r0 + RB, :] = hi
            pl_ref[s, T + r0:T + r0 + RB, :] = (p - hi.astype(f32)).astype(bf16)
            if r0 == 0:
                first = p[0:1, :]
        return first

    def sb_suffix(s):
        cum_ref[s] = jnp.dot(tri2_ref[...], pl_ref[s], preferred_element_type=f32)

    def sb_weights(s, carry, first, masked):
        for r0 in range(0, T, RB):
            w = jnp.exp(lb_ref[s, r0:r0 + RB, :] - cum_ref[s, r0:r0 + RB, :] + carry)
            if masked:
                w = jnp.where(causal_rows(r0), w, 0.0)
            w_ref[s, r0:r0 + RB, :] = w.astype(bf16)
        return carry - (cum_ref[s, 0:1, :] + first)

    def sb_values(j, hd, s):
        sbacc_ref[hd] += jnp.dot(sbvt_ref[j, hd], w_ref[s], preferred_element_type=f32)

    def sb_chunks(chunks, carries):
        steps = [(j, masked, hd) for (j, masked) in chunks for hd in range(SB_HEADS)]
        carries = list(carries)
        firsts = [None] * len(steps)

        def finish(k):
            j, masked, hd = steps[k]
            carries[hd] = sb_weights(k % 2, carries[hd], firsts[k], masked)
            sb_values(j, hd, k % 2)

        sb_scores(steps[0][0], steps[0][2], 0)
        for k, (j, masked, hd) in enumerate(steps):
            if k + 1 < len(steps):
                sb_scores(steps[k + 1][0], steps[k + 1][2], (k + 1) % 2)
            firsts[k] = sb_logs(k % 2, masked)
            sb_suffix(k % 2)
            if k >= 1:
                finish(k - 1)
        finish(len(steps) - 1)
        return tuple(carries)

    zero_carries = tuple(jnp.zeros((1, T), f32) for _ in range(SB_HEADS))
    first_chunks = [(qi, True), (qi - 1, False)] if qi >= 1 else [(qi, True)]
    carries = sb_chunks(first_chunks, zero_carries)

    def carry_max(cs):
        return jnp.max(functools.reduce(jnp.maximum, cs))

    def sb_cond(state):
        i, cmax, _ = state
        return (i < nchunk) & (cmax >= EXP_ZERO_BELOW)

    def sb_body(state):
        i, _, cs = state
        cs = sb_chunks([(qi - i, False)], cs)
        return i + 1, carry_max(cs), cs

    if nchunk > 2:
        lax.while_loop(sb_cond, sb_body, (jnp.int32(2), carry_max(carries), carries))
    for hd in range(SB_HEADS):
        mixt_ref[hd * HEAD_DIM:(hd + 1) * HEAD_DIM, :] = sbacc_ref[hd]

    def mem_logits(hd):
        return lax.dot_general(memk_ref[hd], memq_ref[hd], NT, preferred_element_type=f32)

    ahead = mem_logits(0)
    for hd in range(MEM_HEADS):
        lg = ahead
        if hd + 1 < MEM_HEADS:
            ahead = mem_logits(hd + 1)
        e = jnp.exp(lg - jnp.max(lg, axis=0, keepdims=True))
        p = e / jnp.sum(e, axis=0, keepdims=True)
        lo = SB_W + DSA_W + hd * HEAD_DIM
        mixt_ref[lo:lo + HEAD_DIM, :] = jnp.dot(memvt_ref[hd], p.astype(bf16), preferred_element_type=f32)

    gated = (mixt_ref[...].T * gate_ref[...]).astype(bf16)
    y = jnp.dot(gated, wout_ref[...], preferred_element_type=f32)
    ms = jnp.mean(y * y, axis=-1, keepdims=True)
    out_ref[...] = x_ref[...] + y * lax.rsqrt(ms + RMS_EPS) * postg_ref[...]


def _attn_kernel_entry(*refs, n_in, aliased, **static):
    rest = refs[n_in + 1:] if aliased else refs[n_in:]
    _attn_kernel(*refs[:n_in], *rest, **static)


def _attention(x, proj, memk, memvt, wuvt_h, wout_bf, post_g, bias, T, topk):
    sbq, sbk, sbvt, qlat, ckv, ckvt, iq, idxk, iwt, memq, gate = proj
    B, S, D = x.shape
    nq = S // T
    out = None
    for qi in range(nq):
        nk = qi + 1
        qrow = lambda b, qi=qi: (b, qi, 0)
        qhead = lambda b, qi=qi: (b, 0, qi, 0)
        krow = lambda b: (b, 0, 0)
        khead = lambda b: (b, 0, 0, 0)
        in_specs = [
            pl.BlockSpec((3, DSA_HEADS, T, T), lambda b: (0, 0, 0, 0)),
            pl.BlockSpec((None, T, D), qrow),
            pl.BlockSpec((None, SB_HEADS, T, HEAD_DIM), qhead),
            pl.BlockSpec((None, SB_HEADS, nk * T, HEAD_DIM), khead),
            pl.BlockSpec((None, nk, SB_HEADS, HEAD_DIM, T), lambda b: (b, 0, 0, 0, 0)),
            pl.BlockSpec((None, DSA_HEADS, T, KV_RANK), qhead),
            pl.BlockSpec((None, nk * T, KV_RANK), krow),
            pl.BlockSpec((None, nk, KV_AUG, T), khead),
            pl.BlockSpec((None, IDX_HEADS, T, IDX_DIM), qhead),
            pl.BlockSpec((None, nk * T, IDX_DIM), krow),
            pl.BlockSpec((None, IDX_HEADS, T), lambda b, qi=qi: (b, 0, qi)),
            pl.BlockSpec((None, MEM_HEADS, T, HEAD_DIM), qhead),
            pl.BlockSpec((None, MEM_HEADS, N_MEM, HEAD_DIM), khead),
            pl.BlockSpec((None, MEM_HEADS, HEAD_DIM, N_MEM), khead),
            pl.BlockSpec((None, T, MIX_W), qrow),
            pl.BlockSpec((DSA_HEADS, HEAD_DIM, KV_RANK), lambda b: (0, 0, 0)),
            pl.BlockSpec((MIX_W, D), lambda b: (0, 0)),
            pl.BlockSpec((1, D), lambda b: (0, 0)),
        ]
        args = [bias, x, sbq, sbk, sbvt, qlat, ckv, ckvt, iq, idxk, iwt, memq, memk, memvt, gate,
                wuvt_h, wout_bf, post_g.reshape(1, D)]
        n_in = len(args)
        aliases = {}
        if out is not None:
            in_specs.append(pl.BlockSpec(memory_space=pl.ANY))
            args.append(out)
            aliases = {n_in: 0}
        scratch = [
            pltpu.VMEM((nk, T, T), jnp.int32),
            pltpu.VMEM((nk, T, T), jnp.int16),
            pltpu.VMEM((nk, T, T), jnp.int16),
            pltpu.VMEM((nk, T, DSA_HEADS * T), f32),
            pltpu.VMEM((T, 2 * T), bf16),
            pltpu.VMEM((T, T), bf16),
            pltpu.VMEM((DSA_HEADS, KV_AUG, T), f32),
            pltpu.VMEM((2, T, T), f32),
            pltpu.VMEM((2, T, T), f32),
            pltpu.VMEM((2, 2 * T, T), bf16),
            pltpu.VMEM((2, T, T), f32),
            pltpu.VMEM((2, T, T), bf16),
            pltpu.VMEM((SB_HEADS, HEAD_DIM, T), f32),
            pltpu.VMEM((MIX_W, T), f32),
        ]
        out = pl.pallas_call(
            functools.partial(_attn_kernel_entry, n_in=n_in, aliased=bool(aliases), T=T, topk=topk, qi=qi),
            grid=(B,), in_specs=in_specs,
            out_specs=pl.BlockSpec((None, T, D), qrow),
            out_shape=jax.ShapeDtypeStruct((B, S, D), f32),
            scratch_shapes=scratch, input_output_aliases=aliases, name=f"hybrid_attn_q{qi}",
            compiler_params=pltpu.CompilerParams(
                dimension_semantics=("arbitrary",), vmem_limit_bytes=VMEM_LIMIT_BYTES),
        )(*args)
    return out


def _tile_sizes(S):
    T = 256
    assert S % T == 0 and T >= MAX_DISTANCE
    tm = 512 if S % 512 == 0 else T
    return tm, T


def kernel(x, mem, pre_norm_g, post_norm_g, w_in, w_uk, w_uv, kv_norm_g, w_mem_kv, w_out, rel_bias):
    B, S, D = x.shape
    assert D == D_MODEL and mem.shape[1] == N_MEM
    topk = min(TOPK_MAX, S // 4)
    tm, T = _tile_sizes(S)
    bias = _bias_tiles(rel_bias, T)
    for layer in range(w_in.shape[0]):
        w_packed = _pack_w_in(w_in[layer])
        wuk_t = jnp.transpose(w_uk[layer], (1, 2, 0)).astype(bf16)
        wuvt_h = jnp.transpose(w_uv[layer], (1, 2, 0)).astype(bf16)
        proj = _project(x, pre_norm_g[layer], w_packed, wuk_t, kv_norm_g[layer], tm, T)
        memk, memvt = _mem_kv(mem, w_mem_kv[layer].astype(bf16))
        x = _attention(x, proj, memk, memvt, wuvt_h, w_out[layer].astype(bf16), post_norm_g[layer],
                       bias, T, topk)
    return x
```

```python
import functools
import math

import jax
import jax.numpy as jnp
from jax import lax
from jax.experimental import pallas as pl
from jax.experimental.pallas import tpu as pltpu

D_MODEL = 1024
N_MEM = 256
HEAD_DIM = 64
SB_HEADS = 6
DSA_HEADS = 6
MEM_HEADS = 4
SB_W = SB_HEADS * HEAD_DIM
DSA_W = DSA_HEADS * HEAD_DIM
MEM_W = MEM_HEADS * HEAD_DIM
MIX_W = SB_W + DSA_W + MEM_W
KV_RANK = 128
IDX_HEADS = 8
IDX_DIM = 32
TOPK_MAX = 256
N_BUCKETS = 32
MAX_DISTANCE = 128
RMS_EPS = 1e-6

LANES = 128
SUBLANES = 8
BF16_ROWS = 16
ATTN_SCALE = HEAD_DIM ** -0.5
IDX_SCALE = (IDX_HEADS * IDX_DIM) ** -0.5
INT_MIN = -2 ** 31
NEG_BIG = -1e30
EXP_ZERO_BELOW = -105.0
VMEM_LIMIT_BYTES = 56 * 1024 * 1024
KV_AUG = KV_RANK + BF16_ROWS

OFF_SBQ = 0
OFF_SBK = OFF_SBQ + SB_W
OFF_SBV = OFF_SBK + SB_W
OFF_DSAQ = OFF_SBV + SB_W
OFF_CKV = OFF_DSAQ + DSA_W
OFF_IQ = OFF_CKV + KV_RANK
OFF_IKW = OFF_IQ + IDX_HEADS * IDX_DIM
OFF_MEMQ = OFF_IKW + LANES
OFF_GATE = OFF_MEMQ + MEM_W
PACKED_COLS = OFF_GATE + MIX_W

f32 = jnp.float32
bf16 = jnp.bfloat16
NT = (((1,), (1,)), ((), ()))


def _pack_w_in(w):
    o = 0
    parts = {}
    for name, n in (("sbq", SB_W), ("sbk", SB_W), ("sbv", SB_W), ("sbg", SB_W), ("dsaq", DSA_W),
                    ("ckv", KV_RANK), ("dsag", DSA_W), ("iq", IDX_HEADS * IDX_DIM), ("ik", IDX_DIM),
                    ("iw", IDX_HEADS), ("memq", MEM_W), ("memg", MEM_W)):
        parts[name] = w[:, o:o + n]
        o += n
    pad = jnp.zeros((w.shape[0], LANES - IDX_DIM - IDX_HEADS), w.dtype)
    packed = jnp.concatenate(
        [parts["sbq"], parts["sbk"], parts["sbv"], parts["dsaq"], parts["ckv"], parts["iq"],
         parts["ik"], parts["iw"], pad, parts["memq"], parts["sbg"], parts["dsag"], parts["memg"]], axis=1)
    assert packed.shape[1] == PACKED_COLS
    return packed.astype(bf16)


def _proj_kernel(x_ref, g_ref, w_ref, wuk_ref, kvg_ref,
                 sbq_ref, sbk_ref, sbvt_ref, qlat_ref, ckv_ref, ckvt_ref, iq_ref, idxk_ref, iwt_ref,
                 memq_ref, gate_ref, *, T):
    tm = x_ref.shape[0]
    x = x_ref[...]
    ms = jnp.mean(x * x, axis=-1, keepdims=True)
    h = (x * lax.rsqrt(ms + RMS_EPS) * g_ref[...]).astype(bf16)

    def seg(off, n):
        return jnp.dot(h, w_ref[:, off:off + n], preferred_element_type=f32)

    a = seg(OFF_SBQ, 2 * SB_W)
    for hd in range(SB_HEADS):
        lo = hd * HEAD_DIM
        sbq_ref[hd] = (a[:, lo:lo + HEAD_DIM] * ATTN_SCALE).astype(bf16)
        sbk_ref[hd] = a[:, SB_W + lo:SB_W + lo + HEAD_DIM].astype(bf16)

    vt = seg(OFF_SBV, SB_W).T
    for r in range(tm // T):
        for hd in range(SB_HEADS):
            sbvt_ref[r, hd] = vt[hd * HEAD_DIM:(hd + 1) * HEAD_DIM, r * T:(r + 1) * T].astype(bf16)

    dq = seg(OFF_DSAQ, DSA_W)
    for hd in range(DSA_HEADS):
        q = dq[:, hd * HEAD_DIM:(hd + 1) * HEAD_DIM].astype(bf16)
        ql = jnp.dot(q, wuk_ref[hd], preferred_element_type=f32)
        qlat_ref[hd] = (ql * ATTN_SCALE).astype(bf16)

    c = seg(OFF_CKV, KV_RANK)
    cms = jnp.mean(c * c, axis=-1, keepdims=True)
    cn = c * lax.rsqrt(cms + RMS_EPS) * kvg_ref[...]
    ckv_ref[...] = cn.astype(bf16)
    cnt = cn.T
    for r in range(tm // T):
        ckvt_ref[r, :KV_RANK, :] = cnt[:, r * T:(r + 1) * T].astype(bf16)
        ckvt_ref[r, KV_RANK:, :] = jnp.ones((BF16_ROWS, T), bf16)

    e = seg(OFF_IQ, IDX_HEADS * IDX_DIM)
    for hd in range(IDX_HEADS):
        iq_ref[hd] = e[:, hd * IDX_DIM:(hd + 1) * IDX_DIM].astype(bf16)

    kw = seg(OFF_IKW, LANES)
    idxk_ref[...] = kw[:, :IDX_DIM].astype(bf16)
    iwt_ref[...] = kw.T[IDX_DIM:IDX_DIM + IDX_HEADS, :] * IDX_SCALE

    mq = seg(OFF_MEMQ, MEM_W)
    for hd in range(MEM_HEADS):
        memq_ref[hd] = (mq[:, hd * HEAD_DIM:(hd + 1) * HEAD_DIM] * ATTN_SCALE).astype(bf16)

    gt = seg(OFF_GATE, MIX_W)
    gate_ref[...] = gt * jax.nn.sigmoid(gt)


def _project(x, pre_g, w_packed, wuk_t, kv_g, tm, T):
    B, S, D = x.shape
    grid = (B, S // tm)
    rpt = tm // T
    row = lambda b, i: (b, i, 0)
    head = lambda b, i: (b, 0, i, 0)
    const2 = lambda b, i: (0, 0)
    const3 = lambda b, i: (0, 0, 0)
    out_shape = (
        jax.ShapeDtypeStruct((B, SB_HEADS, S, HEAD_DIM), bf16),
        jax.ShapeDtypeStruct((B, SB_HEADS, S, HEAD_DIM), bf16),
        jax.ShapeDtypeStruct((B, S // T, SB_HEADS, HEAD_DIM, T), bf16),
        jax.ShapeDtypeStruct((B, DSA_HEADS, S, KV_RANK), bf16),
        jax.ShapeDtypeStruct((B, S, KV_RANK), bf16),
        jax.ShapeDtypeStruct((B, S // T, KV_AUG, T), bf16),
        jax.ShapeDtypeStruct((B, IDX_HEADS, S, IDX_DIM), bf16),
        jax.ShapeDtypeStruct((B, S, IDX_DIM), bf16),
        jax.ShapeDtypeStruct((B, IDX_HEADS, S), f32),
        jax.ShapeDtypeStruct((B, MEM_HEADS, S, HEAD_DIM), bf16),
        jax.ShapeDtypeStruct((B, S, MIX_W), f32),
    )
    out_specs = (
        pl.BlockSpec((None, SB_HEADS, tm, HEAD_DIM), head),
        pl.BlockSpec((None, SB_HEADS, tm, HEAD_DIM), head),
        pl.BlockSpec((None, rpt, SB_HEADS, HEAD_DIM, T), lambda b, i: (b, i, 0, 0, 0)),
        pl.BlockSpec((None, DSA_HEADS, tm, KV_RANK), head),
        pl.BlockSpec((None, tm, KV_RANK), row),
        pl.BlockSpec((None, rpt, KV_AUG, T), lambda b, i: (b, i, 0, 0)),
        pl.BlockSpec((None, IDX_HEADS, tm, IDX_DIM), head),
        pl.BlockSpec((None, tm, IDX_DIM), row),
        pl.BlockSpec((None, IDX_HEADS, tm), lambda b, i: (b, 0, i)),
        pl.BlockSpec((None, MEM_HEADS, tm, HEAD_DIM), head),
        pl.BlockSpec((None, tm, MIX_W), row),
    )
    in_specs = [
        pl.BlockSpec((None, tm, D), row),
        pl.BlockSpec((1, D), const2),
        pl.BlockSpec((D, PACKED_COLS), const2),
        pl.BlockSpec((DSA_HEADS, HEAD_DIM, KV_RANK), const3),
        pl.BlockSpec((1, KV_RANK), const2),
    ]
    return pl.pallas_call(
        functools.partial(_proj_kernel, T=T),
        grid=grid, in_specs=in_specs, out_specs=out_specs, out_shape=out_shape,
        name="in_proj",
        compiler_params=pltpu.CompilerParams(
            dimension_semantics=("arbitrary", "arbitrary"), vmem_limit_bytes=VMEM_LIMIT_BYTES),
    )(x, pre_g.reshape(1, D), w_packed, wuk_t, kv_g.reshape(1, KV_RANK))


def _memkv_kernel(mem_ref, w_ref, k_ref, vt_ref):
    m = mem_ref[...].astype(bf16)
    kv = jnp.dot(m, w_ref[...], preferred_element_type=f32)
    vt = kv[:, MEM_W:].T
    for hd in range(MEM_HEADS):
        lo = hd * HEAD_DIM
        k_ref[hd] = kv[:, lo:lo + HEAD_DIM].astype(bf16)
        vt_ref[hd] = vt[lo:lo + HEAD_DIM, :].astype(bf16)


def _mem_kv(mem, w_mem_kv_bf):
    B, M, D = mem.shape
    out_shape = (jax.ShapeDtypeStruct((B, MEM_HEADS, M, HEAD_DIM), bf16),
                 jax.ShapeDtypeStruct((B, MEM_HEADS, HEAD_DIM, M), bf16))
    return pl.pallas_call(
        _memkv_kernel, grid=(B,),
        in_specs=[pl.BlockSpec((None, M, D), lambda b: (b, 0, 0)),
                  pl.BlockSpec((D, 2 * MEM_W), lambda b: (0, 0))],
        out_specs=(pl.BlockSpec((None, MEM_HEADS, M, HEAD_DIM), lambda b: (b, 0, 0, 0)),
                   pl.BlockSpec((None, MEM_HEADS, HEAD_DIM, M), lambda b: (b, 0, 0, 0))),
        out_shape=out_shape, name="mem_kv",
        compiler_params=pltpu.CompilerParams(
            dimension_semantics=("arbitrary",), vmem_limit_bytes=VMEM_LIMIT_BYTES),
    )(mem, w_mem_kv_bf)


def _t5_bucket(n):
    max_exact = N_BUCKETS // 2
    nf = jnp.maximum(n, 1).astype(f32)
    large = max_exact + (jnp.log(nf / max_exact) / math.log(MAX_DISTANCE / max_exact)
                         * (N_BUCKETS - max_exact)).astype(jnp.int32)
    large = jnp.minimum(large, N_BUCKETS - 1)
    return jnp.where(n < max_exact, n, large)


def _bias_kernel(relb_ref, bias_ref, *, T):
    key_l = lax.broadcasted_iota(jnp.int32, (T, T), 0)
    qry_l = lax.broadcasted_iota(jnp.int32, (T, T), 1)
    for kind in range(3):
        bucket = _t5_bucket(jnp.maximum(kind * T + qry_l - key_l, 0))
        for hd in range(DSA_HEADS):
            tile = jnp.zeros((T, T), f32)
            for k in range(N_BUCKETS):
                tile = jnp.where(bucket == k, relb_ref[k, hd], tile)
            bias_ref[kind, hd] = tile


def _bias_tiles(rel_bias, T):
    return pl.pallas_call(
        functools.partial(_bias_kernel, T=T),
        in_specs=[pl.BlockSpec(memory_space=pltpu.SMEM)],
        out_shape=jax.ShapeDtypeStruct((3, DSA_HEADS, T, T), f32), name="t5_bias",
        compiler_params=pltpu.CompilerParams(vmem_limit_bytes=VMEM_LIMIT_BYTES),
    )(rel_bias)


def _fold_keys(v, op):
    t = v.shape[0]
    v3 = v.reshape(t // SUBLANES, SUBLANES, v.shape[1])
    return op(v3, axis=0)


def _loop(lo, hi, body, init):
    if isinstance(lo, int) and isinstance(hi, int):
        val = init
        for i in range(lo, hi):
            val = body(i, val)
        return val
    return lax.fori_loop(lo, hi, body, init)


def _fold_rows16(m):
    parts = [m[i * BF16_ROWS:(i + 1) * BF16_ROWS] for i in range(m.shape[0] // BF16_ROWS)]
    while len(parts) > 1:
        parts = [parts[i] + parts[i + 1] for i in range(0, len(parts), 2)]
    return parts[0]


def _attn_kernel(bias_ref, x_ref, sbq_ref, sbk_ref, sbvt_ref, qlat_ref, ckv_ref, ckvt_ref, iq_ref, idxk_ref,
                 iwt_ref, memq_ref, memk_ref, memvt_ref, gate_ref, wuvt_ref, wout_ref, postg_ref,
                 out_ref,
                 keys_ref, k16_ref, lo16_ref, lg_ref, tri2_ref, tri_lt_ref, acc_ref,
                 z_ref, lb_ref, pl_ref, cum_ref, w_ref, sbacc_ref, mixt_ref,
                 *, T, topk, qi):
    b = pl.program_id(0)
    nchunk = qi + 1
    i16 = jnp.int16
    key_l = lax.broadcasted_iota(jnp.int32, (T, T), 0)
    qry_l = lax.broadcasted_iota(jnp.int32, (T, T), 1)

    @pl.when(b == 0)
    def _init():
        later = jnp.where(qry_l > key_l, 1.0, 0.0).astype(bf16)
        tri2_ref[:, :T] = later
        tri2_ref[:, T:] = later
        tri_lt_ref[...] = jnp.where(qry_l < key_l, 1.0, 0.0).astype(bf16)

    iq2d = iq_ref[...].reshape(IDX_HEADS * T, IDX_DIM)
    iwt = iwt_ref[...]

    def score_body(j, carry):
        ks = j * T
        dots = lax.dot_general(idxk_ref[pl.ds(ks, T), :], iq2d, NT, preferred_element_type=f32)
        score = jnp.zeros((T, T), f32)
        for hd in range(IDX_HEADS):
            score = score + iwt[hd:hd + 1, :] * jnp.maximum(dots[:, hd * T:(hd + 1) * T], 0.0)
        bits = pltpu.bitcast(score, jnp.int32)
        key = jnp.where(bits < 0, INT_MIN - bits, bits)
        if j == qi:
            key = jnp.where(key_l <= qry_l, key, INT_MIN)
        keys_ref[j] = key
        k16_ref[j] = (key >> 16).astype(i16)
        return carry

    _loop(0, nchunk, score_body, 0)

    def count16(ref, cand, strict=False):
        c16 = cand.astype(i16)

        def body(j, acc):
            k = ref[j]
            hit = (k > c16) if strict else (k >= c16)
            return acc + _fold_rows16(jnp.where(hit, i16(1), i16(0)))

        acc = _loop(0, nchunk, body, jnp.zeros((BF16_ROWS, T), i16))
        return jnp.sum(acc.astype(f32), axis=0, keepdims=True)

    kf = float(topk)
    half = 1 << 15
    n_all = float(nchunk * T)
    n_pos = count16(k16_ref, jnp.zeros((1, T), jnp.int32))
    hi0 = jnp.where(n_pos >= kf, 0, -half).astype(jnp.int32)
    n0 = jnp.where(n_pos >= kf, n_pos, n_all)

    def hi_body(i, carry):
        hi, n_hi = carry
        cand = hi | (jnp.int32(1) << (14 - i))
        n = count16(k16_ref, cand)
        ok = n >= kf
        return jnp.where(ok, cand, hi), jnp.where(ok, n, n_hi)

    thr_hi, n_hi = lax.fori_loop(0, 15, hi_body, (hi0, n0))
    n_above = count16(k16_ref, thr_hi, strict=True)

    def lo_fill(j, carry):
        key = keys_ref[j]
        low = (key & 0xFFFF) - half
        lo16_ref[j] = jnp.where((key >> 16) == thr_hi, low, -half).astype(i16)
        return carry

    _loop(0, nchunk, lo_fill, 0)

    def lo_body(i, carry):
        lo, n_lo = carry
        cand = lo | (jnp.int32(1) << (15 - i))
        n = n_above + count16(lo16_ref, cand - half)
        ok = n >= kf
        return jnp.where(ok, cand, lo), jnp.where(ok, n, n_lo)

    thr_lo, n_ge = lax.fori_loop(0, 16, lo_body, (jnp.zeros((1, T), jnp.int32), n_hi))
    thr = thr_hi * (1 << 16) + thr_lo
    has_ties = jnp.max(jnp.where((n_ge > kf) & (thr > INT_MIN), 1.0, 0.0)) > 0.0
    thr_floor = jnp.maximum(thr, INT_MIN + 1)

    def count_above_thr(_):
        def body(j, acc):
            return acc + _fold_keys(jnp.where(keys_ref[j] > thr, 1.0, 0.0), jnp.sum)
        acc = _loop(0, nchunk, body, jnp.zeros((SUBLANES, T), f32))
        return jnp.sum(acc, axis=0, keepdims=True)

    need = kf - lax.cond(has_ties, count_above_thr, lambda _: jnp.zeros((1, T), f32), 0)

    def dsa_logits_walk(with_ties):
        def dsa_logits(j, carry):
            eq_seen, maxes = carry
            k = keys_ref[j]
            if with_ties:
                eq = jnp.where(k == thr, 1.0, 0.0)
                prefix = jnp.dot(tri_lt_ref[...], eq.astype(bf16), preferred_element_type=f32) + eq_seen
                bump = jnp.where(prefix >= need, 1, 0).astype(jnp.int32)
                sel = k >= jnp.maximum(thr + bump, INT_MIN + 1)
                eq_seen = eq_seen + jnp.sum(_fold_keys(eq, jnp.sum), axis=0, keepdims=True)
            else:
                sel = k >= thr_floor
            ckv = ckv_ref[pl.ds(j * T, T), :]
            kind = min(qi - j, 2)

            def head_logits(hd):
                return lax.dot_general(ckv, qlat_ref[hd], NT, preferred_element_type=f32)

            new_maxes = []
            ahead = head_logits(0)
            for hd in range(DSA_HEADS):
                logits = ahead
                if hd + 1 < DSA_HEADS:
                    ahead = head_logits(hd + 1)
                lg = jnp.where(sel, logits + bias_ref[kind, hd], NEG_BIG)
                lg_ref[j, :, hd * T:(hd + 1) * T] = lg
                new_maxes.append(jnp.maximum(maxes[hd], _fold_keys(lg, jnp.max)))
            return eq_seen, tuple(new_maxes)

        def walk(init_max):
            return _loop(0, nchunk, dsa_logits, (jnp.zeros((1, T), f32), init_max))[1]
        return walk

    init_max = tuple(jnp.full((SUBLANES, T), NEG_BIG, f32) for _ in range(DSA_HEADS))
    maxes = lax.cond(has_ties, dsa_logits_walk(True), dsa_logits_walk(False), init_max)
    row_max = [jnp.max(m, axis=0, keepdims=True) for m in maxes]

    acc_ref[...] = jnp.zeros(acc_ref.shape, f32)

    def dsa_values(j, carry):
        ckvt = ckvt_ref[j]
        for hd in range(DSA_HEADS):
            p = jnp.exp(lg_ref[j, :, hd * T:(hd + 1) * T] - row_max[hd]).astype(bf16)
            acc_ref[hd] += jnp.dot(ckvt, p, preferred_element_type=f32)
        return carry

    _loop(0, nchunk, dsa_values, 0)

    for hd in range(DSA_HEADS):
        a = acc_ref[hd]
        o_lat = (a[:KV_RANK] / a[KV_RANK:KV_RANK + 1]).astype(bf16)
        mixt_ref[SB_W + hd * HEAD_DIM:SB_W + (hd + 1) * HEAD_DIM, :] = jnp.dot(
            wuvt_ref[hd], o_lat, preferred_element_type=f32)

    def mem_logits(hd):
        return lax.dot_general(memk_ref[hd], memq_ref[hd], NT, preferred_element_type=f32)

    ahead = mem_logits(0)
    for hd in range(MEM_HEADS):
        lg = ahead
        if hd + 1 < MEM_HEADS:
            ahead = mem_logits(hd + 1)
        e = jnp.exp(lg - jnp.max(lg, axis=0, keepdims=True))
        p = e / jnp.sum(e, axis=0, keepdims=True)
        lo = SB_W + DSA_W + hd * HEAD_DIM
        mixt_ref[lo:lo + HEAD_DIM, :] = jnp.dot(memvt_ref[hd], p.astype(bf16), preferred_element_type=f32)

    sbacc_ref[...] = jnp.zeros(sbacc_ref.shape, f32)
    RB = 64
    SB_SLOTS = z_ref.shape[0]
    sign_bit = jnp.int32(INT_MIN)

    def causal_rows(r0):
        return tri2_ref[r0:r0 + RB, :T].astype(f32)

    def sb_scores(j, hd, s):
        ks = j * T if isinstance(j, int) else pl.multiple_of(j * T, T)
        z_ref[s] = lax.dot_general(sbk_ref[hd, pl.ds(ks, T), :], sbq_ref[hd], NT, preferred_element_type=f32)

    def sb_logs(s, masked):
        first = None
        for r0 in range(0, T, RB):
            z = z_ref[s, r0:r0 + RB, :]
            neg_abs = pltpu.bitcast(pltpu.bitcast(z, jnp.int32) | sign_bit, f32)
            p = jnp.maximum(z, 0.0) + jnp.log(1.0 + jnp.exp(neg_abs))
            lb_ref[s, r0:r0 + RB, :] = z - p
            if masked:
                p = p * causal_rows(r0)
            hi = p.astype(bf16)
            pl_ref[s, r0:r0 + RB, :] = hi
            pl_ref[s, T + r0:T + r0 + RB, :] = (p - hi.astype(f32)).astype(bf16)
            if r0 == 0:
                first = p[0:1, :]
        return first

    def sb_suffix(s):
        cum_ref[s] = jnp.dot(tri2_ref[...], pl_ref[s], preferred_element_type=f32)

    def sb_weights(s, carry, first, masked):
        for r0 in range(0, T, RB):
            w = jnp.exp(lb_ref[s, r0:r0 + RB, :] - cum_ref[s, r0:r0 + RB, :] + carry)
            if masked:
                w = w * causal_rows(r0)
            w_ref[s, r0:r0 + RB, :] = w.astype(bf16)
        return carry - (cum_ref[s, 0:1, :] + first)

    def sb_values(j, hd, s):
        sbacc_ref[hd] += jnp.dot(sbvt_ref[j, hd], w_ref[s], preferred_element_type=f32)

    def sb_chunks(chunks, carries):
        steps = [(j, masked, hd) for (j, masked) in chunks for hd in range(SB_HEADS)]
        n = len(steps)
        carries = list(carries)
        firsts = [None] * n

        def scores(k):
            sb_scores(steps[k][0], steps[k][2], k % SB_SLOTS)

        def finish(k):
            j, masked, hd = steps[k]
            carries[hd] = sb_weights(k % SB_SLOTS, carries[hd], firsts[k], masked)
            sb_values(j, hd, k % SB_SLOTS)

        for k in range(min(2, n)):
            scores(k)
        for t in range(n + 2):
            if t + 2 < n:
                scores(t + 2)
            if t < n:
                firsts[t] = sb_logs(t % SB_SLOTS, steps[t][1])
                sb_suffix(t % SB_SLOTS)
            if 0 <= t - 2 < n:
                finish(t - 2)
        return tuple(carries)

    zero_carries = tuple(jnp.zeros((1, T), f32) for _ in range(SB_HEADS))
    first_chunks = [(qi, True), (qi - 1, False)] if qi >= 1 else [(qi, True)]
    carries = sb_chunks(first_chunks, zero_carries)

    def carry_max(cs):
        return jnp.max(functools.reduce(jnp.maximum, cs))

    def sb_cond(state):
        i, cmax, _ = state
        return (i < nchunk) & (cmax >= EXP_ZERO_BELOW)

    def sb_body(state):
        i, _, cs = state
        cs = sb_chunks([(qi - i, False)], cs)
        return i + 1, carry_max(cs), cs

    if nchunk > 2:
        lax.while_loop(sb_cond, sb_body, (jnp.int32(2), carry_max(carries), carries))
    for hd in range(SB_HEADS):
        mixt_ref[hd * HEAD_DIM:(hd + 1) * HEAD_DIM, :] = sbacc_ref[hd]

    gated = (mixt_ref[...].T * gate_ref[...]).astype(bf16)
    y = jnp.dot(gated, wout_ref[...], preferred_element_type=f32)
    ms = jnp.mean(y * y, axis=-1, keepdims=True)
    out_ref[...] = x_ref[...] + y * lax.rsqrt(ms + RMS_EPS) * postg_ref[...]


def _attn_kernel_entry(*refs, n_in, aliased, **static):
    rest = refs[n_in + 1:] if aliased else refs[n_in:]
    _attn_kernel(*refs[:n_in], *rest, **static)


def _attention(x, proj, memk, memvt, wuvt_h, wout_bf, post_g, bias, T, topk):
    sbq, sbk, sbvt, qlat, ckv, ckvt, iq, idxk, iwt, memq, gate = proj
    B, S, D = x.shape
    nq = S // T
    out = None
    for qi in range(nq):
        nk = qi + 1
        qrow = lambda b, qi=qi: (b, qi, 0)
        qhead = lambda b, qi=qi: (b, 0, qi, 0)
        krow = lambda b: (b, 0, 0)
        khead = lambda b: (b, 0, 0, 0)
        in_specs = [
            pl.BlockSpec((3, DSA_HEADS, T, T), lambda b: (0, 0, 0, 0)),
            pl.BlockSpec((None, T, D), qrow),
            pl.BlockSpec((None, SB_HEADS, T, HEAD_DIM), qhead),
            pl.BlockSpec((None, SB_HEADS, nk * T, HEAD_DIM), khead),
            pl.BlockSpec((None, nk, SB_HEADS, HEAD_DIM, T), lambda b: (b, 0, 0, 0, 0)),
            pl.BlockSpec((None, DSA_HEADS, T, KV_RANK), qhead),
            pl.BlockSpec((None, nk * T, KV_RANK), krow),
            pl.BlockSpec((None, nk, KV_AUG, T), khead),
            pl.BlockSpec((None, IDX_HEADS, T, IDX_DIM), qhead),
            pl.BlockSpec((None, nk * T, IDX_DIM), krow),
            pl.BlockSpec((None, IDX_HEADS, T), lambda b, qi=qi: (b, 0, qi)),
            pl.BlockSpec((None, MEM_HEADS, T, HEAD_DIM), qhead),
            pl.BlockSpec((None, MEM_HEADS, N_MEM, HEAD_DIM), khead),
            pl.BlockSpec((None, MEM_HEADS, HEAD_DIM, N_MEM), khead),
            pl.BlockSpec((None, T, MIX_W), qrow),
            pl.BlockSpec((DSA_HEADS, HEAD_DIM, KV_RANK), lambda b: (0, 0, 0)),
            pl.BlockSpec((MIX_W, D), lambda b: (0, 0)),
            pl.BlockSpec((1, D), lambda b: (0, 0)),
        ]
        args = [bias, x, sbq, sbk, sbvt, qlat, ckv, ckvt, iq, idxk, iwt, memq, memk, memvt, gate,
                wuvt_h, wout_bf, post_g.reshape(1, D)]
        n_in = len(args)
        aliases = {}
        if out is not None:
            in_specs.append(pl.BlockSpec(memory_space=pl.ANY))
            args.append(out)
            aliases = {n_in: 0}
        scratch = [
            pltpu.VMEM((nk, T, T), jnp.int32),
            pltpu.VMEM((nk, T, T), jnp.int16),
            pltpu.VMEM((nk, T, T), jnp.int16),
            pltpu.VMEM((nk, T, DSA_HEADS * T), f32),
            pltpu.VMEM((T, 2 * T), bf16),
            pltpu.VMEM((T, T), bf16),
            pltpu.VMEM((DSA_HEADS, KV_AUG, T), f32),
            pltpu.VMEM((3, T, T), f32),
            pltpu.VMEM((3, T, T), f32),
            pltpu.VMEM((3, 2 * T, T), bf16),
            pltpu.VMEM((3, T, T), f32),
            pltpu.VMEM((3, T, T), bf16),
            pltpu.VMEM((SB_HEADS, HEAD_DIM, T), f32),
            pltpu.VMEM((MIX_W, T), f32),
        ]
        out = pl.pallas_call(
            functools.partial(_attn_kernel_entry, n_in=n_in, aliased=bool(aliases), T=T, topk=topk, qi=qi),
            grid=(B,), in_specs=in_specs,
            out_specs=pl.BlockSpec((None, T, D), qrow),
            out_shape=jax.ShapeDtypeStruct((B, S, D), f32),
            scratch_shapes=scratch, input_output_aliases=aliases, name=f"hybrid_attn_q{qi}",
            compiler_params=pltpu.CompilerParams(
                dimension_semantics=("arbitrary",), vmem_limit_bytes=VMEM_LIMIT_BYTES),
        )(*args)
    return out


def _tile_sizes(S):
    T = 256
    assert S % T == 0 and T >= MAX_DISTANCE
    tm = 512 if S % 512 == 0 else T
    return tm, T


def kernel(x, mem, pre_norm_g, post_norm_g, w_in, w_uk, w_uv, kv_norm_g, w_mem_kv, w_out, rel_bias):
    B, S, D = x.shape
    assert D == D_MODEL and mem.shape[1] == N_MEM
    topk = min(TOPK_MAX, S // 4)
    tm, T = _tile_sizes(S)
    bias = _bias_tiles(rel_bias, T)
    for layer in range(w_in.shape[0]):
        w_packed = _pack_w_in(w_in[layer])
        wuk_t = jnp.transpose(w_uk[layer], (1, 2, 0)).astype(bf16)
        wuvt_h = jnp.transpose(w_uv[layer], (1, 2, 0)).astype(bf16)
        proj = _project(x, pre_norm_g[layer], w_packed, wuk_t, kv_norm_g[layer], tm, T)
        memk, memvt = _mem_kv(mem, w_mem_kv[layer].astype(bf16))
        x = _attention(x, proj, memk, memvt, wuvt_h, w_out[layer].astype(bf16), post_norm_g[layer],
                       bias, T, topk)
    return x
```

```python
import functools
import math

import jax
import jax.numpy as jnp
from jax import lax
from jax.experimental import pallas as pl
from jax.experimental.pallas import tpu as pltpu

D_MODEL = 1024
N_MEM = 256
HEAD_DIM = 64
SB_HEADS = 6
DSA_HEADS = 6
MEM_HEADS = 4
SB_W = SB_HEADS * HEAD_DIM
DSA_W = DSA_HEADS * HEAD_DIM
MEM_W = MEM_HEADS * HEAD_DIM
MIX_W = SB_W + DSA_W + MEM_W
KV_RANK = 128
IDX_HEADS = 8
IDX_DIM = 32
TOPK_MAX = 256
N_BUCKETS = 32
MAX_DISTANCE = 128
RMS_EPS = 1e-6

LANES = 128
SUBLANES = 8
BF16_ROWS = 16
ATTN_SCALE = HEAD_DIM ** -0.5
IDX_SCALE = (IDX_HEADS * IDX_DIM) ** -0.5
INT_MIN = -2 ** 31
NEG_BIG = -1e30
EXP_ZERO_BELOW = -105.0
VMEM_LIMIT_BYTES = 56 * 1024 * 1024
KV_AUG = KV_RANK + BF16_ROWS

OFF_SBQ = 0
OFF_SBK = OFF_SBQ + SB_W
OFF_SBV = OFF_SBK + SB_W
OFF_DSAQ = OFF_SBV + SB_W
OFF_CKV = OFF_DSAQ + DSA_W
OFF_IQ = OFF_CKV + KV_RANK
OFF_IKW = OFF_IQ + IDX_HEADS * IDX_DIM
OFF_MEMQ = OFF_IKW + LANES
OFF_GATE = OFF_MEMQ + MEM_W
PACKED_COLS = OFF_GATE + MIX_W

f32 = jnp.float32
bf16 = jnp.bfloat16
NT = (((1,), (1,)), ((), ()))


def _pack_w_in(w):
    o = 0
    parts = {}
    for name, n in (("sbq", SB_W), ("sbk", SB_W), ("sbv", SB_W), ("sbg", SB_W), ("dsaq", DSA_W),
                    ("ckv", KV_RANK), ("dsag", DSA_W), ("iq", IDX_HEADS * IDX_DIM), ("ik", IDX_DIM),
                    ("iw", IDX_HEADS), ("memq", MEM_W), ("memg", MEM_W)):
        parts[name] = w[:, o:o + n]
        o += n
    pad = jnp.zeros((w.shape[0], LANES - IDX_DIM - IDX_HEADS), w.dtype)
    packed = jnp.concatenate(
        [parts["sbq"], parts["sbk"], parts["sbv"], parts["dsaq"], parts["ckv"], parts["iq"],
         parts["ik"], parts["iw"], pad, parts["memq"], parts["sbg"], parts["dsag"], parts["memg"]], axis=1)
    assert packed.shape[1] == PACKED_COLS
    return packed.astype(bf16)


def _proj_kernel(x_ref, g_ref, w_ref, wuk_ref, kvg_ref,
                 sbq_ref, sbk_ref, sbvt_ref, qlat_ref, ckv_ref, ckvt_ref, iq_ref, idxk_ref, iwt_ref,
                 memq_ref, gate_ref, *, T):
    tm = x_ref.shape[0]
    x = x_ref[...]
    ms = jnp.mean(x * x, axis=-1, keepdims=True)
    h = (x * lax.rsqrt(ms + RMS_EPS) * g_ref[...]).astype(bf16)

    def seg(off, n):
        return jnp.dot(h, w_ref[:, off:off + n], preferred_element_type=f32)

    a = seg(OFF_SBQ, 2 * SB_W)
    for hd in range(SB_HEADS):
        lo = hd * HEAD_DIM
        sbq_ref[hd] = (a[:, lo:lo + HEAD_DIM] * ATTN_SCALE).astype(bf16)
        sbk_ref[hd] = a[:, SB_W + lo:SB_W + lo + HEAD_DIM].astype(bf16)

    vt = seg(OFF_SBV, SB_W).T
    for r in range(tm // T):
        for hd in range(SB_HEADS):
            sbvt_ref[r, hd] = vt[hd * HEAD_DIM:(hd + 1) * HEAD_DIM, r * T:(r + 1) * T].astype(bf16)

    dq = seg(OFF_DSAQ, DSA_W)
    for hd in range(DSA_HEADS):
        q = dq[:, hd * HEAD_DIM:(hd + 1) * HEAD_DIM].astype(bf16)
        ql = jnp.dot(q, wuk_ref[hd], preferred_element_type=f32)
        qlat_ref[hd] = (ql * ATTN_SCALE).astype(bf16)

    c = seg(OFF_CKV, KV_RANK)
    cms = jnp.mean(c * c, axis=-1, keepdims=True)
    cn = c * lax.rsqrt(cms + RMS_EPS) * kvg_ref[...]
    ckv_ref[...] = cn.astype(bf16)
    cnt = cn.T
    for r in range(tm // T):
        ckvt_ref[r, :KV_RANK, :] = cnt[:, r * T:(r + 1) * T].astype(bf16)
        ckvt_ref[r, KV_RANK:, :] = jnp.ones((BF16_ROWS, T), bf16)

    e = seg(OFF_IQ, IDX_HEADS * IDX_DIM)
    for hd in range(IDX_HEADS):
        iq_ref[hd] = e[:, hd * IDX_DIM:(hd + 1) * IDX_DIM].astype(bf16)

    kw = seg(OFF_IKW, LANES)
    idxk_ref[...] = kw[:, :IDX_DIM].astype(bf16)
    iwt_ref[...] = kw.T[IDX_DIM:IDX_DIM + IDX_HEADS, :] * IDX_SCALE

    mq = seg(OFF_MEMQ, MEM_W)
    for hd in range(MEM_HEADS):
        memq_ref[hd] = (mq[:, hd * HEAD_DIM:(hd + 1) * HEAD_DIM] * ATTN_SCALE).astype(bf16)

    gt = seg(OFF_GATE, MIX_W)
    gate_ref[...] = gt * jax.nn.sigmoid(gt)


def _project(x, pre_g, w_packed, wuk_t, kv_g, tm, T):
    B, S, D = x.shape
    grid = (B, S // tm)
    rpt = tm // T
    row = lambda b, i: (b, i, 0)
    head = lambda b, i: (b, 0, i, 0)
    const2 = lambda b, i: (0, 0)
    const3 = lambda b, i: (0, 0, 0)
    out_shape = (
        jax.ShapeDtypeStruct((B, SB_HEADS, S, HEAD_DIM), bf16),
        jax.ShapeDtypeStruct((B, SB_HEADS, S, HEAD_DIM), bf16),
        jax.ShapeDtypeStruct((B, S // T, SB_HEADS, HEAD_DIM, T), bf16),
        jax.ShapeDtypeStruct((B, DSA_HEADS, S, KV_RANK), bf16),
        jax.ShapeDtypeStruct((B, S, KV_RANK), bf16),
        jax.ShapeDtypeStruct((B, S // T, KV_AUG, T), bf16),
        jax.ShapeDtypeStruct((B, IDX_HEADS, S, IDX_DIM), bf16),
        jax.ShapeDtypeStruct((B, S, IDX_DIM), bf16),
        jax.ShapeDtypeStruct((B, IDX_HEADS, S), f32),
        jax.ShapeDtypeStruct((B, MEM_HEADS, S, HEAD_DIM), bf16),
        jax.ShapeDtypeStruct((B, S, MIX_W), f32),
    )
    out_specs = (
        pl.BlockSpec((None, SB_HEADS, tm, HEAD_DIM), head),
        pl.BlockSpec((None, SB_HEADS, tm, HEAD_DIM), head),
        pl.BlockSpec((None, rpt, SB_HEADS, HEAD_DIM, T), lambda b, i: (b, i, 0, 0, 0)),
        pl.BlockSpec((None, DSA_HEADS, tm, KV_RANK), head),
        pl.BlockSpec((None, tm, KV_RANK), row),
        pl.BlockSpec((None, rpt, KV_AUG, T), lambda b, i: (b, i, 0, 0)),
        pl.BlockSpec((None, IDX_HEADS, tm, IDX_DIM), head),
        pl.BlockSpec((None, tm, IDX_DIM), row),
        pl.BlockSpec((None, IDX_HEADS, tm), lambda b, i: (b, 0, i)),
        pl.BlockSpec((None, MEM_HEADS, tm, HEAD_DIM), head),
        pl.BlockSpec((None, tm, MIX_W), row),
    )
    in_specs = [
        pl.BlockSpec((None, tm, D), row),
        pl.BlockSpec((1, D), const2),
        pl.BlockSpec((D, PACKED_COLS), const2),
        pl.BlockSpec((DSA_HEADS, HEAD_DIM, KV_RANK), const3),
        pl.BlockSpec((1, KV_RANK), const2),
    ]
    return pl.pallas_call(
        functools.partial(_proj_kernel, T=T),
        grid=grid, in_specs=in_specs, out_specs=out_specs, out_shape=out_shape,
        name="in_proj",
        compiler_params=pltpu.CompilerParams(
            dimension_semantics=("arbitrary", "arbitrary"), vmem_limit_bytes=VMEM_LIMIT_BYTES),
    )(x, pre_g.reshape(1, D), w_packed, wuk_t, kv_g.reshape(1, KV_RANK))


def _memkv_kernel(mem_ref, w_ref, k_ref, vt_ref):
    m = mem_ref[...].astype(bf16)
    kv = jnp.dot(m, w_ref[...], preferred_element_type=f32)
    vt = kv[:, MEM_W:].T
    for hd in range(MEM_HEADS):
        lo = hd * HEAD_DIM
        k_ref[hd] = kv[:, lo:lo + HEAD_DIM].astype(bf16)
        vt_ref[hd] = vt[lo:lo + HEAD_DIM, :].astype(bf16)


def _mem_kv(mem, w_mem_kv_bf):
    B, M, D = mem.shape
    out_shape = (jax.ShapeDtypeStruct((B, MEM_HEADS, M, HEAD_DIM), bf16),
                 jax.ShapeDtypeStruct((B, MEM_HEADS, HEAD_DIM, M), bf16))
    return pl.pallas_call(
        _memkv_kernel, grid=(B,),
        in_specs=[pl.BlockSpec((None, M, D), lambda b: (b, 0, 0)),
                  pl.BlockSpec((D, 2 * MEM_W), lambda b: (0, 0))],
        out_specs=(pl.BlockSpec((None, MEM_HEADS, M, HEAD_DIM), lambda b: (b, 0, 0, 0)),
                   pl.BlockSpec((None, MEM_HEADS, HEAD_DIM, M), lambda b: (b, 0, 0, 0))),
        out_shape=out_shape, name="mem_kv",
        compiler_params=pltpu.CompilerParams(
            dimension_semantics=("arbitrary",), vmem_limit_bytes=VMEM_LIMIT_BYTES),
    )(mem, w_mem_kv_bf)


def _t5_bucket(n):
    max_exact = N_BUCKETS // 2
    nf = jnp.maximum(n, 1).astype(f32)
    large = max_exact + (jnp.log(nf / max_exact) / math.log(MAX_DISTANCE / max_exact)
                         * (N_BUCKETS - max_exact)).astype(jnp.int32)
    large = jnp.minimum(large, N_BUCKETS - 1)
    return jnp.where(n < max_exact, n, large)


def _bias_kernel(relb_ref, bias_ref, *, T):
    key_l = lax.broadcasted_iota(jnp.int32, (T, T), 0)
    qry_l = lax.broadcasted_iota(jnp.int32, (T, T), 1)
    for kind in range(3):
        bucket = _t5_bucket(jnp.maximum(kind * T + qry_l - key_l, 0))
        for hd in range(DSA_HEADS):
            tile = jnp.zeros((T, T), f32)
            for k in range(N_BUCKETS):
                tile = jnp.where(bucket == k, relb_ref[k, hd], tile)
            bias_ref[kind, hd] = tile


def _bias_tiles(rel_bias, T):
    return pl.pallas_call(
        functools.partial(_bias_kernel, T=T),
        in_specs=[pl.BlockSpec(memory_space=pltpu.SMEM)],
        out_shape=jax.ShapeDtypeStruct((3, DSA_HEADS, T, T), f32), name="t5_bias",
        compiler_params=pltpu.CompilerParams(vmem_limit_bytes=VMEM_LIMIT_BYTES),
    )(rel_bias)


def _fold_keys(v, op):
    t = v.shape[0]
    v3 = v.reshape(t // SUBLANES, SUBLANES, v.shape[1])
    return op(v3, axis=0)


def _loop(lo, hi, body, init):
    if isinstance(lo, int) and isinstance(hi, int):
        val = init
        for i in range(lo, hi):
            val = body(i, val)
        return val
    return lax.fori_loop(lo, hi, body, init)


def _fold_rows16(m):
    parts = [m[i * BF16_ROWS:(i + 1) * BF16_ROWS] for i in range(m.shape[0] // BF16_ROWS)]
    while len(parts) > 1:
        parts = [parts[i] + parts[i + 1] for i in range(0, len(parts), 2)]
    return parts[0]


def _attn_kernel(bias_ref, x_ref, sbq_ref, sbk_ref, sbvt_ref, qlat_ref, ckv_ref, ckvt_ref, iq_ref, idxk_ref,
                 iwt_ref, memq_ref, memk_ref, memvt_ref, gate_ref, wuvt_ref, wout_ref, postg_ref,
                 out_ref,
                 keys_ref, k16_ref, lo16_ref, lg_ref, tri2_ref, tri_lt_ref, acc_ref,
                 z_ref, lb_ref, pl_ref, cum_ref, w_ref, sbacc_ref, mixt_ref,
                 *, T, topk, qi):
    b = pl.program_id(0)
    nchunk = qi + 1
    i16 = jnp.int16
    key_l = lax.broadcasted_iota(jnp.int32, (T, T), 0)
    qry_l = lax.broadcasted_iota(jnp.int32, (T, T), 1)

    @pl.when(b == 0)
    def _init():
        later = jnp.where(qry_l > key_l, 1.0, 0.0).astype(bf16)
        tri2_ref[:, :T] = later
        tri2_ref[:, T:] = later
        tri_lt_ref[...] = jnp.where(qry_l < key_l, 1.0, 0.0).astype(bf16)

    iq2d = iq_ref[...].reshape(IDX_HEADS * T, IDX_DIM)
    iwt = iwt_ref[...]

    def score_body(j, carry):
        ks = j * T
        dots = lax.dot_general(idxk_ref[pl.ds(ks, T), :], iq2d, NT, preferred_element_type=f32)
        score = jnp.zeros((T, T), f32)
        for hd in range(IDX_HEADS):
            score = score + iwt[hd:hd + 1, :] * jnp.maximum(dots[:, hd * T:(hd + 1) * T], 0.0)
        bits = pltpu.bitcast(score, jnp.int32)
        key = jnp.where(bits < 0, INT_MIN - bits, bits)
        if j == qi:
            key = jnp.where(key_l <= qry_l, key, INT_MIN)
        keys_ref[j] = key
        k16_ref[j] = (key >> 16).astype(i16)
        return carry

    _loop(0, nchunk, score_body, 0)

    def count16(ref, cand, strict=False):
        c16 = cand.astype(i16)

        def body(j, acc):
            k = ref[j]
            hit = (k > c16) if strict else (k >= c16)
            return acc + _fold_rows16(jnp.where(hit, i16(1), i16(0)))

        acc = _loop(0, nchunk, body, jnp.zeros((BF16_ROWS, T), i16))
        return jnp.sum(acc.astype(f32), axis=0, keepdims=True)

    kf = float(topk)
    half = 1 << 15
    n_all = float(nchunk * T)
    n_pos = count16(k16_ref, jnp.zeros((1, T), jnp.int32))
    hi0 = jnp.where(n_pos >= kf, 0, -half).astype(jnp.int32)
    n0 = jnp.where(n_pos >= kf, n_pos, n_all)

    def hi_body(i, carry):
        hi, n_hi = carry
        cand = hi | (jnp.int32(1) << (14 - i))
        n = count16(k16_ref, cand)
        ok = n >= kf
        return jnp.where(ok, cand, hi), jnp.where(ok, n, n_hi)

    thr_hi, n_hi = lax.fori_loop(0, 15, hi_body, (hi0, n0))
    n_above = count16(k16_ref, thr_hi, strict=True)

    def lo_fill(j, carry):
        key = keys_ref[j]
        low = (key & 0xFFFF) - half
        lo16_ref[j] = jnp.where((key >> 16) == thr_hi, low, -half).astype(i16)
        return carry

    _loop(0, nchunk, lo_fill, 0)

    def lo_body(i, carry):
        lo, n_lo = carry
        cand = lo | (jnp.int32(1) << (15 - i))
        n = n_above + count16(lo16_ref, cand - half)
        ok = n >= kf
        return jnp.where(ok, cand, lo), jnp.where(ok, n, n_lo)

    thr_lo, n_ge = lax.fori_loop(0, 16, lo_body, (jnp.zeros((1, T), jnp.int32), n_hi))
    thr = thr_hi * (1 << 16) + thr_lo
    has_ties = jnp.max(jnp.where((n_ge > kf) & (thr > INT_MIN), 1.0, 0.0)) > 0.0
    thr_floor = jnp.maximum(thr, INT_MIN + 1)

    def count_above_thr(_):
        def body(j, acc):
            return acc + _fold_keys(jnp.where(keys_ref[j] > thr, 1.0, 0.0), jnp.sum)
        acc = _loop(0, nchunk, body, jnp.zeros((SUBLANES, T), f32))
        return jnp.sum(acc, axis=0, keepdims=True)

    need = kf - lax.cond(has_ties, count_above_thr, lambda _: jnp.zeros((1, T), f32), 0)

    def dsa_logits_walk(with_ties):
        def dsa_logits(j, carry):
            eq_seen, maxes = carry
            k = keys_ref[j]
            if with_ties:
                eq = jnp.where(k == thr, 1.0, 0.0)
                prefix = jnp.dot(tri_lt_ref[...], eq.astype(bf16), preferred_element_type=f32) + eq_seen
                bump = jnp.where(prefix >= need, 1, 0).astype(jnp.int32)
                sel = k >= jnp.maximum(thr + bump, INT_MIN + 1)
                eq_seen = eq_seen + jnp.sum(_fold_keys(eq, jnp.sum), axis=0, keepdims=True)
            else:
                sel = k >= thr_floor
            ckv = ckv_ref[pl.ds(j * T, T), :]
            kind = min(qi - j, 2)

            def head_logits(hd):
                return lax.dot_general(ckv, qlat_ref[hd], NT, preferred_element_type=f32)

            new_maxes = []
            ahead = head_logits(0)
            for hd in range(DSA_HEADS):
                logits = ahead
                if hd + 1 < DSA_HEADS:
                    ahead = head_logits(hd + 1)
                lg = jnp.where(sel, logits + bias_ref[kind, hd], NEG_BIG)
                lg_ref[j, :, hd * T:(hd + 1) * T] = lg
                new_maxes.append(jnp.maximum(maxes[hd], _fold_keys(lg, jnp.max)))
            return eq_seen, tuple(new_maxes)

        def walk(init_max):
            return _loop(0, nchunk, dsa_logits, (jnp.zeros((1, T), f32), init_max))[1]
        return walk

    init_max = tuple(jnp.full((SUBLANES, T), NEG_BIG, f32) for _ in range(DSA_HEADS))
    maxes = lax.cond(has_ties, dsa_logits_walk(True), dsa_logits_walk(False), init_max)
    row_max = [jnp.max(m, axis=0, keepdims=True) for m in maxes]

    acc_ref[...] = jnp.zeros(acc_ref.shape, f32)

    def dsa_values(j, carry):
        ckvt = ckvt_ref[j]
        for hd in range(DSA_HEADS):
            p = jnp.exp(lg_ref[j, :, hd * T:(hd + 1) * T] - row_max[hd]).astype(bf16)
            acc_ref[hd] += jnp.dot(ckvt, p, preferred_element_type=f32)
        return carry

    _loop(0, nchunk, dsa_values, 0)

    for hd in range(DSA_HEADS):
        a = acc_ref[hd]
        o_lat = (a[:KV_RANK] / a[KV_RANK:KV_RANK + 1]).astype(bf16)
        mixt_ref[SB_W + hd * HEAD_DIM:SB_W + (hd + 1) * HEAD_DIM, :] = jnp.dot(
            wuvt_ref[hd], o_lat, preferred_element_type=f32)

    def mem_logits(hd):
        return lax.dot_general(memk_ref[hd], memq_ref[hd], NT, preferred_element_type=f32)

    ahead = mem_logits(0)
    for hd in range(MEM_HEADS):
        lg = ahead
        if hd + 1 < MEM_HEADS:
            ahead = mem_logits(hd + 1)
        e = jnp.exp(lg - jnp.max(lg, axis=0, keepdims=True))
        p = e / jnp.sum(e, axis=0, keepdims=True)
        lo = SB_W + DSA_W + hd * HEAD_DIM
        mixt_ref[lo:lo + HEAD_DIM, :] = jnp.dot(memvt_ref[hd], p.astype(bf16), preferred_element_type=f32)

    sbacc_ref[...] = jnp.zeros(sbacc_ref.shape, f32)
    RB = 64
    SB_SLOTS = z_ref.shape[0]
    sign_bit = jnp.int32(INT_MIN)

    def causal_rows(r0):
        return tri2_ref[r0:r0 + RB, :T].astype(f32)

    def sb_scores(j, hd, s):
        ks = j * T if isinstance(j, int) else pl.multiple_of(j * T, T)
        z_ref[s] = lax.dot_general(sbk_ref[hd, pl.ds(ks, T), :], sbq_ref[hd], NT, preferred_element_type=f32)

    def sb_logs(s, masked):
        first = None
        for r0 in range(0, T, RB):
            z = z_ref[s, r0:r0 + RB, :]
            neg_abs = pltpu.bitcast(pltpu.bitcast(z, jnp.int32) | sign_bit, f32)
            p = jnp.maximum(z, 0.0) + jnp.log(1.0 + jnp.exp(neg_abs))
            lb_ref[s, r0:r0 + RB, :] = z - p
            if masked:
                p = p * causal_rows(r0)
            hi = p.astype(bf16)
            pl_ref[s, r0:r0 + RB, :] = hi
            pl_ref[s, T + r0:T + r0 + RB, :] = (p - hi.astype(f32)).astype(bf16)
            if r0 == 0:
                first = p[0:1, :]
        return first

    def sb_suffix(s):
        cum_ref[s] = jnp.dot(tri2_ref[...], pl_ref[s], preferred_element_type=f32)

    def sb_weights(s, carry, first, masked):
        for r0 in range(0, T, RB):
            w = jnp.exp(lb_ref[s, r0:r0 + RB, :] - cum_ref[s, r0:r0 + RB, :] + carry)
            if masked:
                w = w * causal_rows(r0)
            w_ref[s, r0:r0 + RB, :] = w.astype(bf16)
        return carry - (cum_ref[s, 0:1, :] + first)

    def sb_values(j, hd, s):
        sbacc_ref[hd] += jnp.dot(sbvt_ref[j, hd], w_ref[s], preferred_element_type=f32)

    def sb_chunks(chunks, carries):
        steps = [(j, masked, hd) for (j, masked) in chunks for hd in range(SB_HEADS)]
        n = len(steps)
        carries = list(carries)
        firsts = [None] * n

        def scores(k):
            sb_scores(steps[k][0], steps[k][2], k % SB_SLOTS)

        def finish(k):
            j, masked, hd = steps[k]
            carries[hd] = sb_weights(k % SB_SLOTS, carries[hd], firsts[k], masked)
            sb_values(j, hd, k % SB_SLOTS)

        for k in range(min(2, n)):
            scores(k)
        for t in range(n + 2):
            if t + 2 < n:
                scores(t + 2)
            if t < n:
                firsts[t] = sb_logs(t % SB_SLOTS, steps[t][1])
                sb_suffix(t % SB_SLOTS)
            if 0 <= t - 2 < n:
                finish(t - 2)
        return tuple(carries)

    zero_carries = tuple(jnp.zeros((1, T), f32) for _ in range(SB_HEADS))
    first_chunks = [(qi, True), (qi - 1, False)] if qi >= 1 else [(qi, True)]
    carries = sb_chunks(first_chunks, zero_carries)

    def carry_max(cs):
        return jnp.max(functools.reduce(jnp.maximum, cs))

    def sb_cond(state):
        i, cmax, _ = state
        return (i < nchunk) & (cmax >= EXP_ZERO_BELOW)

    def sb_body(state):
        i, _, cs = state
        cs = sb_chunks([(qi - i, False)], cs)
        return i + 1, carry_max(cs), cs

    if nchunk > 2:
        lax.while_loop(sb_cond, sb_body, (jnp.int32(2), carry_max(carries), carries))
    for hd in range(SB_HEADS):
        mixt_ref[hd * HEAD_DIM:(hd + 1) * HEAD_DIM, :] = sbacc_ref[hd]

    gated = (mixt_ref[...].T * gate_ref[...]).astype(bf16)
    y = jnp.dot(gated, wout_ref[...], preferred_element_type=f32)
    ms = jnp.mean(y * y, axis=-1, keepdims=True)
    res = x_ref[...] + y * lax.rsqrt(ms + RMS_EPS) * postg_ref[...]
    if out_ref.shape[0] == T:
        out_ref[...] = res
    else:
        out_ref[qi * T:(qi + 1) * T, :] = res
        for r in range(out_ref.shape[0] // T):
            if r != qi:
                out_ref[r * T:(r + 1) * T, :] = jnp.zeros((T, out_ref.shape[1]), f32)


def _attn_kernel_entry(*refs, n_in, aliased, **static):
    rest = refs[n_in + 1:] if aliased else refs[n_in:]
    _attn_kernel(*refs[:n_in], *rest, **static)


def _attention(x, proj, memk, memvt, wuvt_h, wout_bf, post_g, bias, T, topk):
    sbq, sbk, sbvt, qlat, ckv, ckvt, iq, idxk, iwt, memq, gate = proj
    B, S, D = x.shape
    nq = S // T
    out = None
    for qi in range(nq):
        nk = qi + 1
        qrow = lambda b, qi=qi: (b, qi, 0)
        qhead = lambda b, qi=qi: (b, 0, qi, 0)
        krow = lambda b: (b, 0, 0)
        khead = lambda b: (b, 0, 0, 0)
        in_specs = [
            pl.BlockSpec((3, DSA_HEADS, T, T), lambda b: (0, 0, 0, 0)),
            pl.BlockSpec((None, T, D), qrow),
            pl.BlockSpec((None, SB_HEADS, T, HEAD_DIM), qhead),
            pl.BlockSpec((None, SB_HEADS, nk * T, HEAD_DIM), khead),
            pl.BlockSpec((None, nk, SB_HEADS, HEAD_DIM, T), lambda b: (b, 0, 0, 0, 0)),
            pl.BlockSpec((None, DSA_HEADS, T, KV_RANK), qhead),
            pl.BlockSpec((None, nk * T, KV_RANK), krow),
            pl.BlockSpec((None, nk, KV_AUG, T), khead),
            pl.BlockSpec((None, IDX_HEADS, T, IDX_DIM), qhead),
            pl.BlockSpec((None, nk * T, IDX_DIM), krow),
            pl.BlockSpec((None, IDX_HEADS, T), lambda b, qi=qi: (b, 0, qi)),
            pl.BlockSpec((None, MEM_HEADS, T, HEAD_DIM), qhead),
            pl.BlockSpec((None, MEM_HEADS, N_MEM, HEAD_DIM), khead),
            pl.BlockSpec((None, MEM_HEADS, HEAD_DIM, N_MEM), khead),
            pl.BlockSpec((None, T, MIX_W), qrow),
            pl.BlockSpec((DSA_HEADS, HEAD_DIM, KV_RANK), lambda b: (0, 0, 0)),
            pl.BlockSpec((MIX_W, D), lambda b: (0, 0)),
            pl.BlockSpec((1, D), lambda b: (0, 0)),
        ]
        args = [bias, x, sbq, sbk, sbvt, qlat, ckv, ckvt, iq, idxk, iwt, memq, memk, memvt, gate,
                wuvt_h, wout_bf, post_g.reshape(1, D)]
        n_in = len(args)
        aliases = {}
        if out is not None:
            in_specs.append(pl.BlockSpec(memory_space=pl.ANY))
            args.append(out)
            aliases = {n_in: 0}
        scratch = [
            pltpu.VMEM((nk, T, T), jnp.int32),
            pltpu.VMEM((nk, T, T), jnp.int16),
            pltpu.VMEM((nk, T, T), jnp.int16),
            pltpu.VMEM((nk, T, DSA_HEADS * T), f32),
            pltpu.VMEM((T, 2 * T), bf16),
            pltpu.VMEM((T, T), bf16),
            pltpu.VMEM((DSA_HEADS, KV_AUG, T), f32),
            pltpu.VMEM((3, T, T), f32),
            pltpu.VMEM((3, T, T), f32),
            pltpu.VMEM((3, 2 * T, T), bf16),
            pltpu.VMEM((3, T, T), f32),
            pltpu.VMEM((3, T, T), bf16),
            pltpu.VMEM((SB_HEADS, HEAD_DIM, T), f32),
            pltpu.VMEM((MIX_W, T), f32),
        ]
        out = pl.pallas_call(
            functools.partial(_attn_kernel_entry, n_in=n_in, aliased=bool(aliases), T=T, topk=topk, qi=qi),
            grid=(B,), in_specs=in_specs,
            out_specs=(pl.BlockSpec((None, T, D), qrow) if aliases
                       else pl.BlockSpec((None, S, D), lambda b: (b, 0, 0))),
            out_shape=jax.ShapeDtypeStruct((B, S, D), f32),
            scratch_shapes=scratch, input_output_aliases=aliases, name=f"hybrid_attn_q{qi}",
            compiler_params=pltpu.CompilerParams(
                dimension_semantics=("arbitrary",), vmem_limit_bytes=VMEM_LIMIT_BYTES),
        )(*args)
    return out


def _tile_sizes(S):
    T = 256
    assert S % T == 0 and T >= MAX_DISTANCE
    tm = 512 if S % 512 == 0 else T
    return tm, T


def kernel(x, mem, pre_norm_g, post_norm_g, w_in, w_uk, w_uv, kv_norm_g, w_mem_kv, w_out, rel_bias):
    B, S, D = x.shape
    assert D == D_MODEL and mem.shape[1] == N_MEM
    topk = min(TOPK_MAX, S // 4)
    tm, T = _tile_sizes(S)
    bias = _bias_tiles(rel_bias, T)
    for layer in range(w_in.shape[0]):
        w_packed = _pack_w_in(w_in[layer])
        wuk_t = jnp.transpose(w_uk[layer], (1, 2, 0)).astype(bf16)
        wuvt_h = jnp.transpose(w_uv[layer], (1, 2, 0)).astype(bf16)
        proj = _project(x, pre_norm_g[layer], w_packed, wuk_t, kv_norm_g[layer], tm, T)
        memk, memvt = _mem_kv(mem, w_mem_kv[layer].astype(bf16))
        x = _attention(x, proj, memk, memvt, wuvt_h, w_out[layer].astype(bf16), post_norm_g[layer],
                       bias, T, topk)
    return x
```

```python
import functools
import math

import jax
import jax.numpy as jnp
from jax import lax
from jax.experimental import pallas as pl
from jax.experimental.pallas import tpu as pltpu

D_MODEL = 1024
N_MEM = 256
HEAD_DIM = 64
SB_HEADS = 6
DSA_HEADS = 6
MEM_HEADS = 4
SB_W = SB_HEADS * HEAD_DIM
DSA_W = DSA_HEADS * HEAD_DIM
MEM_W = MEM_HEADS * HEAD_DIM
MIX_W = SB_W + DSA_W + MEM_W
KV_RANK = 128
IDX_HEADS = 8
IDX_DIM = 32
TOPK_MAX = 256
N_BUCKETS = 32
MAX_DISTANCE = 128
RMS_EPS = 1e-6

LANES = 128
SUBLANES = 8
BF16_ROWS = 16
ATTN_SCALE = HEAD_DIM ** -0.5
IDX_SCALE = (IDX_HEADS * IDX_DIM) ** -0.5
INT_MIN = -2 ** 31
NEG_BIG = -1e30
EXP_ZERO_BELOW = -105.0
VMEM_LIMIT_BYTES = 56 * 1024 * 1024
KV_AUG = KV_RANK + BF16_ROWS

OFF_SBQ = 0
OFF_SBK = OFF_SBQ + SB_W
OFF_SBV = OFF_SBK + SB_W
OFF_DSAQ = OFF_SBV + SB_W
OFF_CKV = OFF_DSAQ + DSA_W
OFF_IQ = OFF_CKV + KV_RANK
OFF_IKW = OFF_IQ + IDX_HEADS * IDX_DIM
OFF_MEMQ = OFF_IKW + LANES
OFF_GATE = OFF_MEMQ + MEM_W
PACKED_COLS = OFF_GATE + MIX_W

f32 = jnp.float32
bf16 = jnp.bfloat16
NT = (((1,), (1,)), ((), ()))


def _pack_w_in(w):
    o = 0
    parts = {}
    for name, n in (("sbq", SB_W), ("sbk", SB_W), ("sbv", SB_W), ("sbg", SB_W), ("dsaq", DSA_W),
                    ("ckv", KV_RANK), ("dsag", DSA_W), ("iq", IDX_HEADS * IDX_DIM), ("ik", IDX_DIM),
                    ("iw", IDX_HEADS), ("memq", MEM_W), ("memg", MEM_W)):
        parts[name] = w[:, o:o + n]
        o += n
    pad = jnp.zeros((w.shape[0], LANES - IDX_DIM - IDX_HEADS), w.dtype)
    packed = jnp.concatenate(
        [parts["sbq"], parts["sbk"], parts["sbv"], parts["dsaq"], parts["ckv"], parts["iq"],
         parts["ik"], parts["iw"], pad, parts["memq"], parts["sbg"], parts["dsag"], parts["memg"]], axis=1)
    assert packed.shape[1] == PACKED_COLS
    return packed.astype(bf16)


def _proj_kernel(x_ref, g_ref, w_ref, wuk_ref, kvg_ref,
                 sbq_ref, sbk_ref, sbvt_ref, qlat_ref, ckv_ref, ckvt_ref, iq_ref, idxk_ref, iwt_ref,
                 memq_ref, gate_ref, *, T):
    tm = x_ref.shape[0]
    x = x_ref[...]
    ms = jnp.mean(x * x, axis=-1, keepdims=True)
    h = (x * lax.rsqrt(ms + RMS_EPS) * g_ref[...]).astype(bf16)

    def seg(off, n):
        return jnp.dot(h, w_ref[:, off:off + n], preferred_element_type=f32)

    a = seg(OFF_SBQ, 2 * SB_W)
    for hd in range(SB_HEADS):
        lo = hd * HEAD_DIM
        sbq_ref[hd] = (a[:, lo:lo + HEAD_DIM] * ATTN_SCALE).astype(bf16)
        sbk_ref[hd] = a[:, SB_W + lo:SB_W + lo + HEAD_DIM].astype(bf16)

    vt = seg(OFF_SBV, SB_W).T
    for r in range(tm // T):
        for hd in range(SB_HEADS):
            sbvt_ref[r, hd] = vt[hd * HEAD_DIM:(hd + 1) * HEAD_DIM, r * T:(r + 1) * T].astype(bf16)

    dq = seg(OFF_DSAQ, DSA_W)
    for hd in range(DSA_HEADS):
        q = dq[:, hd * HEAD_DIM:(hd + 1) * HEAD_DIM].astype(bf16)
        ql = jnp.dot(q, wuk_ref[hd], preferred_element_type=f32)
        qlat_ref[hd] = (ql * ATTN_SCALE).astype(bf16)

    c = seg(OFF_CKV, KV_RANK)
    cms = jnp.mean(c * c, axis=-1, keepdims=True)
    cn = c * lax.rsqrt(cms + RMS_EPS) * kvg_ref[...]
    ckv_ref[...] = cn.astype(bf16)
    cnt = cn.T
    for r in range(tm // T):
        ckvt_ref[r, :KV_RANK, :] = cnt[:, r * T:(r + 1) * T].astype(bf16)
        ckvt_ref[r, KV_RANK:, :] = jnp.ones((BF16_ROWS, T), bf16)

    e = seg(OFF_IQ, IDX_HEADS * IDX_DIM)
    for hd in range(IDX_HEADS):
        iq_ref[hd] = e[:, hd * IDX_DIM:(hd + 1) * IDX_DIM].astype(bf16)

    kw = seg(OFF_IKW, LANES)
    idxk_ref[...] = kw[:, :IDX_DIM].astype(bf16)
    iwt_ref[...] = kw.T[IDX_DIM:IDX_DIM + IDX_HEADS, :] * IDX_SCALE

    mq = seg(OFF_MEMQ, MEM_W)
    for hd in range(MEM_HEADS):
        memq_ref[hd] = (mq[:, hd * HEAD_DIM:(hd + 1) * HEAD_DIM] * ATTN_SCALE).astype(bf16)

    gt = seg(OFF_GATE, MIX_W)
    gate_ref[...] = gt * jax.nn.sigmoid(gt)


def _project(x, pre_g, w_packed, wuk_t, kv_g, tm, T):
    B, S, D = x.shape
    grid = (B, S // tm)
    rpt = tm // T
    row = lambda b, i: (b, i, 0)
    head = lambda b, i: (b, 0, i, 0)
    const2 = lambda b, i: (0, 0)
    const3 = lambda b, i: (0, 0, 0)
    out_shape = (
        jax.ShapeDtypeStruct((B, SB_HEADS, S, HEAD_DIM), bf16),
        jax.ShapeDtypeStruct((B, SB_HEADS, S, HEAD_DIM), bf16),
        jax.ShapeDtypeStruct((B, S // T, SB_HEADS, HEAD_DIM, T), bf16),
        jax.ShapeDtypeStruct((B, DSA_HEADS, S, KV_RANK), bf16),
        jax.ShapeDtypeStruct((B, S, KV_RANK), bf16),
        jax.ShapeDtypeStruct((B, S // T, KV_AUG, T), bf16),
        jax.ShapeDtypeStruct((B, IDX_HEADS, S, IDX_DIM), bf16),
        jax.ShapeDtypeStruct((B, S, IDX_DIM), bf16),
        jax.ShapeDtypeStruct((B, IDX_HEADS, S), f32),
        jax.ShapeDtypeStruct((B, MEM_HEADS, S, HEAD_DIM), bf16),
        jax.ShapeDtypeStruct((B, S, MIX_W), f32),
    )
    out_specs = (
        pl.BlockSpec((None, SB_HEADS, tm, HEAD_DIM), head),
        pl.BlockSpec((None, SB_HEADS, tm, HEAD_DIM), head),
        pl.BlockSpec((None, rpt, SB_HEADS, HEAD_DIM, T), lambda b, i: (b, i, 0, 0, 0)),
        pl.BlockSpec((None, DSA_HEADS, tm, KV_RANK), head),
        pl.BlockSpec((None, tm, KV_RANK), row),
        pl.BlockSpec((None, rpt, KV_AUG, T), lambda b, i: (b, i, 0, 0)),
        pl.BlockSpec((None, IDX_HEADS, tm, IDX_DIM), head),
        pl.BlockSpec((None, tm, IDX_DIM), row),
        pl.BlockSpec((None, IDX_HEADS, tm), lambda b, i: (b, 0, i)),
        pl.BlockSpec((None, MEM_HEADS, tm, HEAD_DIM), head),
        pl.BlockSpec((None, tm, MIX_W), row),
    )
    in_specs = [
        pl.BlockSpec((None, tm, D), row),
        pl.BlockSpec((1, D), const2),
        pl.BlockSpec((D, PACKED_COLS), const2),
        pl.BlockSpec((DSA_HEADS, HEAD_DIM, KV_RANK), const3),
        pl.BlockSpec((1, KV_RANK), const2),
    ]
    return pl.pallas_call(
        functools.partial(_proj_kernel, T=T),
        grid=grid, in_specs=in_specs, out_specs=out_specs, out_shape=out_shape,
        name="in_proj",
        compiler_params=pltpu.CompilerParams(
            dimension_semantics=("arbitrary", "arbitrary"), vmem_limit_bytes=VMEM_LIMIT_BYTES),
    )(x, pre_g.reshape(1, D), w_packed, wuk_t, kv_g.reshape(1, KV_RANK))


def _memkv_kernel(mem_ref, w_ref, k_ref, vt_ref):
    m = mem_ref[...].astype(bf16)
    kv = jnp.dot(m, w_ref[...], preferred_element_type=f32)
    vt = kv[:, MEM_W:].T
    for hd in range(MEM_HEADS):
        lo = hd * HEAD_DIM
        k_ref[hd] = kv[:, lo:lo + HEAD_DIM].astype(bf16)
        vt_ref[hd] = vt[lo:lo + HEAD_DIM, :].astype(bf16)


def _mem_kv(mem, w_mem_kv_bf):
    B, M, D = mem.shape
    out_shape = (jax.ShapeDtypeStruct((B, MEM_HEADS, M, HEAD_DIM), bf16),
                 jax.ShapeDtypeStruct((B, MEM_HEADS, HEAD_DIM, M), bf16))
    return pl.pallas_call(
        _memkv_kernel, grid=(B,),
        in_specs=[pl.BlockSpec((None, M, D), lambda b: (b, 0, 0)),
                  pl.BlockSpec((D, 2 * MEM_W), lambda b: (0, 0))],
        out_specs=(pl.BlockSpec((None, MEM_HEADS, M, HEAD_DIM), lambda b: (b, 0, 0, 0)),
                   pl.BlockSpec((None, MEM_HEADS, HEAD_DIM, M), lambda b: (b, 0, 0, 0))),
        out_shape=out_shape, name="mem_kv",
        compiler_params=pltpu.CompilerParams(
            dimension_semantics=("arbitrary",), vmem_limit_bytes=VMEM_LIMIT_BYTES),
    )(mem, w_mem_kv_bf)


def _t5_bucket(n):
    max_exact = N_BUCKETS // 2
    nf = jnp.maximum(n, 1).astype(f32)
    large = max_exact + (jnp.log(nf / max_exact) / math.log(MAX_DISTANCE / max_exact)
                         * (N_BUCKETS - max_exact)).astype(jnp.int32)
    large = jnp.minimum(large, N_BUCKETS - 1)
    return jnp.where(n < max_exact, n, large)


def _bias_kernel(relb_ref, bias_ref, *, T):
    key_l = lax.broadcasted_iota(jnp.int32, (T, T), 0)
    qry_l = lax.broadcasted_iota(jnp.int32, (T, T), 1)
    for kind in range(3):
        bucket = _t5_bucket(jnp.maximum(kind * T + qry_l - key_l, 0))
        for hd in range(DSA_HEADS):
            tile = jnp.zeros((T, T), f32)
            for k in range(N_BUCKETS):
                tile = jnp.where(bucket == k, relb_ref[k, hd], tile)
            bias_ref[kind, hd] = tile


def _bias_tiles(rel_bias, T):
    return pl.pallas_call(
        functools.partial(_bias_kernel, T=T),
        in_specs=[pl.BlockSpec(memory_space=pltpu.SMEM)],
        out_shape=jax.ShapeDtypeStruct((3, DSA_HEADS, T, T), f32), name="t5_bias",
        compiler_params=pltpu.CompilerParams(vmem_limit_bytes=VMEM_LIMIT_BYTES),
    )(rel_bias)


def _fold_keys(v, op):
    t = v.shape[0]
    v3 = v.reshape(t // SUBLANES, SUBLANES, v.shape[1])
    return op(v3, axis=0)


def _loop(lo, hi, body, init):
    if isinstance(lo, int) and isinstance(hi, int):
        val = init
        for i in range(lo, hi):
            val = body(i, val)
        return val
    return lax.fori_loop(lo, hi, body, init)


def _fold_rows16(m):
    parts = [m[i * BF16_ROWS:(i + 1) * BF16_ROWS] for i in range(m.shape[0] // BF16_ROWS)]
    while len(parts) > 1:
        parts = [parts[i] + parts[i + 1] for i in range(0, len(parts), 2)]
    return parts[0]


def _attn_kernel(bias_ref, x_ref, sbq_ref, sbk_ref, sbvt_ref, qlat_ref, ckv_ref, ckvt_ref, iq_ref, idxk_ref,
                 iwt_ref, memq_ref, memk_ref, memvt_ref, gate_ref, wuvt_ref, wout_ref, postg_ref,
                 out_ref,
                 keys_ref, k16_ref, lo16_ref, lg_ref, tri2_ref, tri_lt_ref, acc_ref,
                 z_ref, lb_ref, pl_ref, cum_ref, w_ref, sbacc_ref, mixt_ref,
                 *, T, topk, qi):
    b = pl.program_id(0)
    nchunk = qi + 1
    i16 = jnp.int16
    key_l = lax.broadcasted_iota(jnp.int32, (T, T), 0)
    qry_l = lax.broadcasted_iota(jnp.int32, (T, T), 1)

    @pl.when(b == 0)
    def _init():
        later = jnp.where(qry_l > key_l, 1.0, 0.0).astype(bf16)
        tri2_ref[:, :T] = later
        tri2_ref[:, T:] = later
        tri_lt_ref[...] = jnp.where(qry_l < key_l, 1.0, 0.0).astype(bf16)

    iq2d = iq_ref[...].reshape(IDX_HEADS * T, IDX_DIM)
    iwt = iwt_ref[...]

    def score_body(j, carry):
        ks = j * T
        dots = lax.dot_general(idxk_ref[pl.ds(ks, T), :], iq2d, NT, preferred_element_type=f32)
        score = jnp.zeros((T, T), f32)
        for hd in range(IDX_HEADS):
            score = score + iwt[hd:hd + 1, :] * jnp.maximum(dots[:, hd * T:(hd + 1) * T], 0.0)
        bits = pltpu.bitcast(score, jnp.int32)
        key = jnp.where(bits < 0, INT_MIN - bits, bits)
        if j == qi:
            key = jnp.where(key_l <= qry_l, key, INT_MIN)
        keys_ref[j] = key
        k16_ref[j] = (key >> 16).astype(i16)
        return carry

    _loop(0, nchunk, score_body, 0)

    def count16(ref, cand, strict=False):
        c16 = cand.astype(i16)

        def body(j, acc):
            k = ref[j]
            hit = (k > c16) if strict else (k >= c16)
            return acc + _fold_rows16(jnp.where(hit, i16(1), i16(0)))

        acc = _loop(0, nchunk, body, jnp.zeros((BF16_ROWS, T), i16))
        return jnp.sum(acc.astype(f32), axis=0, keepdims=True)

    kf = float(topk)
    half = 1 << 15
    n_all = float(nchunk * T)
    n_pos = count16(k16_ref, jnp.zeros((1, T), jnp.int32))
    hi0 = jnp.where(n_pos >= kf, 0, -half).astype(jnp.int32)
    n0 = jnp.where(n_pos >= kf, n_pos, n_all)

    def hi_body(i, carry):
        hi, n_hi = carry
        cand = hi | (jnp.int32(1) << (14 - i))
        n = count16(k16_ref, cand)
        ok = n >= kf
        return jnp.where(ok, cand, hi), jnp.where(ok, n, n_hi)

    thr_hi, n_hi = lax.fori_loop(0, 15, hi_body, (hi0, n0))
    n_above = count16(k16_ref, thr_hi, strict=True)

    def lo_fill(j, carry):
        key = keys_ref[j]
        low = (key & 0xFFFF) - half
        lo16_ref[j] = jnp.where((key >> 16) == thr_hi, low, -half).astype(i16)
        return carry

    _loop(0, nchunk, lo_fill, 0)

    def lo_body(i, carry):
        lo, n_lo = carry
        cand = lo | (jnp.int32(1) << (15 - i))
        n = n_above + count16(lo16_ref, cand - half)
        ok = n >= kf
        return jnp.where(ok, cand, lo), jnp.where(ok, n, n_lo)

    thr_lo, n_ge = lax.fori_loop(0, 16, lo_body, (jnp.zeros((1, T), jnp.int32), n_hi))
    thr = thr_hi * (1 << 16) + thr_lo
    has_ties = jnp.max(jnp.where((n_ge > kf) & (thr > INT_MIN), 1.0, 0.0)) > 0.0
    thr_floor = jnp.maximum(thr, INT_MIN + 1)

    def count_above_thr(_):
        def body(j, acc):
            return acc + _fold_keys(jnp.where(keys_ref[j] > thr, 1.0, 0.0), jnp.sum)
        acc = _loop(0, nchunk, body, jnp.zeros((SUBLANES, T), f32))
        return jnp.sum(acc, axis=0, keepdims=True)

    need = kf - lax.cond(has_ties, count_above_thr, lambda _: jnp.zeros((1, T), f32), 0)

    def dsa_logits_walk(with_ties):
        def dsa_logits(j, carry):
            eq_seen, maxes = carry
            k = keys_ref[j]
            if with_ties:
                eq = jnp.where(k == thr, 1.0, 0.0)
                prefix = jnp.dot(tri_lt_ref[...], eq.astype(bf16), preferred_element_type=f32) + eq_seen
                bump = jnp.where(prefix >= need, 1, 0).astype(jnp.int32)
                sel = k >= jnp.maximum(thr + bump, INT_MIN + 1)
                eq_seen = eq_seen + jnp.sum(_fold_keys(eq, jnp.sum), axis=0, keepdims=True)
            else:
                sel = k >= thr_floor
            ckv = ckv_ref[pl.ds(j * T, T), :]
            kind = min(qi - j, 2)

            def head_logits(hd):
                return lax.dot_general(ckv, qlat_ref[hd], NT, preferred_element_type=f32)

            new_maxes = []
            ahead = head_logits(0)
            for hd in range(DSA_HEADS):
                logits = ahead
                if hd + 1 < DSA_HEADS:
                    ahead = head_logits(hd + 1)
                lg = jnp.where(sel, logits + bias_ref[kind, hd], NEG_BIG)
                lg_ref[j, :, hd * T:(hd + 1) * T] = lg
                new_maxes.append(jnp.maximum(maxes[hd], _fold_keys(lg, jnp.max)))
            return eq_seen, tuple(new_maxes)

        def walk(init_max):
            return _loop(0, nchunk, dsa_logits, (jnp.zeros((1, T), f32), init_max))[1]
        return walk

    init_max = tuple(jnp.full((SUBLANES, T), NEG_BIG, f32) for _ in range(DSA_HEADS))
    maxes = lax.cond(has_ties, dsa_logits_walk(True), dsa_logits_walk(False), init_max)
    row_max = [jnp.max(m, axis=0, keepdims=True) for m in maxes]

    acc_ref[...] = jnp.zeros(acc_ref.shape, f32)

    def dsa_values(j, carry):
        ckvt = ckvt_ref[j]
        for hd in range(DSA_HEADS):
            p = jnp.exp(lg_ref[j, :, hd * T:(hd + 1) * T] - row_max[hd]).astype(bf16)
            acc_ref[hd] += jnp.dot(ckvt, p, preferred_element_type=f32)
        return carry

    _loop(0, nchunk, dsa_values, 0)

    for hd in range(DSA_HEADS):
        a = acc_ref[hd]
        o_lat = (a[:KV_RANK] / a[KV_RANK:KV_RANK + 1]).astype(bf16)
        mixt_ref[SB_W + hd * HEAD_DIM:SB_W + (hd + 1) * HEAD_DIM, :] = jnp.dot(
            wuvt_ref[hd], o_lat, preferred_element_type=f32)

    def mem_logits(hd):
        return lax.dot_general(memk_ref[hd], memq_ref[hd], NT, preferred_element_type=f32)

    ahead = mem_logits(0)
    for hd in range(MEM_HEADS):
        lg = ahead
        if hd + 1 < MEM_HEADS:
            ahead = mem_logits(hd + 1)
        e = jnp.exp(lg - jnp.max(lg, axis=0, keepdims=True))
        p = e / jnp.sum(e, axis=0, keepdims=True)
        lo = SB_W + DSA_W + hd * HEAD_DIM
        mixt_ref[lo:lo + HEAD_DIM, :] = jnp.dot(memvt_ref[hd], p.astype(bf16), preferred_element_type=f32)

    sbacc_ref[...] = jnp.zeros(sbacc_ref.shape, f32)
    RB = 64
    SB_SLOTS = z_ref.shape[0]
    sign_bit = jnp.int32(INT_MIN)

    def causal_rows(r0):
        return tri2_ref[r0:r0 + RB, :T].astype(f32)

    def sb_scores(j, hd, s):
        ks = j * T if isinstance(j, int) else pl.multiple_of(j * T, T)
        z_ref[s] = lax.dot_general(sbk_ref[hd, pl.ds(ks, T), :], sbq_ref[hd], NT, preferred_element_type=f32)

    def sb_logs(s, masked):
        first = None
        for r0 in range(0, T, RB):
            z = z_ref[s, r0:r0 + RB, :]
            neg_abs = pltpu.bitcast(pltpu.bitcast(z, jnp.int32) | sign_bit, f32)
            p = jnp.maximum(z, 0.0) + jnp.log(1.0 + jnp.exp(neg_abs))
            lb_ref[s, r0:r0 + RB, :] = z - p
            if masked:
                p = p * causal_rows(r0)
            hi = p.astype(bf16)
            pl_ref[s, r0:r0 + RB, :] = hi
            pl_ref[s, T + r0:T + r0 + RB, :] = (p - hi.astype(f32)).astype(bf16)
            if r0 == 0:
                first = p[0:1, :]
        return first

    def sb_suffix(s):
        cum_ref[s] = jnp.dot(tri2_ref[...], pl_ref[s], preferred_element_type=f32)

    def sb_weights(s, carry, first, masked):
        for r0 in range(0, T, RB):
            w = jnp.exp(lb_ref[s, r0:r0 + RB, :] - cum_ref[s, r0:r0 + RB, :] + carry)
            if masked:
                w = w * causal_rows(r0)
            w_ref[s, r0:r0 + RB, :] = w.astype(bf16)
        return carry - (cum_ref[s, 0:1, :] + first)

    def sb_values(j, hd, s):
        sbacc_ref[hd] += jnp.dot(sbvt_ref[j, hd], w_ref[s], preferred_element_type=f32)

    def sb_chunks(chunks, carries):
        steps = [(j, masked, hd) for (j, masked) in chunks for hd in range(SB_HEADS)]
        n = len(steps)
        carries = list(carries)
        firsts = [None] * n

        def scores(k):
            sb_scores(steps[k][0], steps[k][2], k % SB_SLOTS)

        def finish(k):
            j, masked, hd = steps[k]
            carries[hd] = sb_weights(k % SB_SLOTS, carries[hd], firsts[k], masked)
            sb_values(j, hd, k % SB_SLOTS)

        for k in range(min(2, n)):
            scores(k)
        for t in range(n + 2):
            if t + 2 < n:
                scores(t + 2)
            if t < n:
                firsts[t] = sb_logs(t % SB_SLOTS, steps[t][1])
                sb_suffix(t % SB_SLOTS)
            if 0 <= t - 2 < n:
                finish(t - 2)
        return tuple(carries)

    zero_carries = tuple(jnp.zeros((1, T), f32) for _ in range(SB_HEADS))
    first_chunks = [(qi, True), (qi - 1, False)] if qi >= 1 else [(qi, True)]
    carries = sb_chunks(first_chunks, zero_carries)

    def carry_max(cs):
        return jnp.max(functools.reduce(jnp.maximum, cs))

    def sb_cond(state):
        i, cmax, _ = state
        return (i < nchunk) & (cmax >= EXP_ZERO_BELOW)

    def sb_body(state):
        i, _, cs = state
        cs = sb_chunks([(qi - i, False)], cs)
        return i + 1, carry_max(cs), cs

    if nchunk > 2:
        lax.while_loop(sb_cond, sb_body, (jnp.int32(2), carry_max(carries), carries))
    for hd in range(SB_HEADS):
        mixt_ref[hd * HEAD_DIM:(hd + 1) * HEAD_DIM, :] = sbacc_ref[hd]

    gated = (mixt_ref[...].T * gate_ref[...]).astype(bf16)
    y = jnp.dot(gated, wout_ref[...], preferred_element_type=f32)
    ms = jnp.mean(y * y, axis=-1, keepdims=True)
    out_ref[...] = x_ref[...] + y * lax.rsqrt(ms + RMS_EPS) * postg_ref[...]


GATE_OPERAND = 14


def _attention(x, proj, memk, memvt, wuvt_h, wout_bf, post_g, bias, T, topk):
    sbq, sbk, sbvt, qlat, ckv, ckvt, iq, idxk, iwt, memq, gate = proj
    B, S, D = x.shape
    assert gate.shape == (B, S, D) and gate.dtype == x.dtype
    nq = S // T
    out = gate
    for qi in range(nq):
        nk = qi + 1
        qrow = lambda b, qi=qi: (b, qi, 0)
        qhead = lambda b, qi=qi: (b, 0, qi, 0)
        krow = lambda b: (b, 0, 0)
        khead = lambda b: (b, 0, 0, 0)
        in_specs = [
            pl.BlockSpec((3, DSA_HEADS, T, T), lambda b: (0, 0, 0, 0)),
            pl.BlockSpec((None, T, D), qrow),
            pl.BlockSpec((None, SB_HEADS, T, HEAD_DIM), qhead),
            pl.BlockSpec((None, SB_HEADS, nk * T, HEAD_DIM), khead),
            pl.BlockSpec((None, nk, SB_HEADS, HEAD_DIM, T), lambda b: (b, 0, 0, 0, 0)),
            pl.BlockSpec((None, DSA_HEADS, T, KV_RANK), qhead),
            pl.BlockSpec((None, nk * T, KV_RANK), krow),
            pl.BlockSpec((None, nk, KV_AUG, T), khead),
            pl.BlockSpec((None, IDX_HEADS, T, IDX_DIM), qhead),
            pl.BlockSpec((None, nk * T, IDX_DIM), krow),
            pl.BlockSpec((None, IDX_HEADS, T), lambda b, qi=qi: (b, 0, qi)),
            pl.BlockSpec((None, MEM_HEADS, T, HEAD_DIM), qhead),
            pl.BlockSpec((None, MEM_HEADS, N_MEM, HEAD_DIM), khead),
            pl.BlockSpec((None, MEM_HEADS, HEAD_DIM, N_MEM), khead),
            pl.BlockSpec((None, T, MIX_W), qrow),
            pl.BlockSpec((DSA_HEADS, HEAD_DIM, KV_RANK), lambda b: (0, 0, 0)),
            pl.BlockSpec((MIX_W, D), lambda b: (0, 0)),
            pl.BlockSpec((1, D), lambda b: (0, 0)),
        ]
        args = [bias, x, sbq, sbk, sbvt, qlat, ckv, ckvt, iq, idxk, iwt, memq, memk, memvt, out,
                wuvt_h, wout_bf, post_g.reshape(1, D)]
        assert args[GATE_OPERAND] is out
        scratch = [
            pltpu.VMEM((nk, T, T), jnp.int32),
            pltpu.VMEM((nk, T, T), jnp.int16),
            pltpu.VMEM((nk, T, T), jnp.int16),
            pltpu.VMEM((nk, T, DSA_HEADS * T), f32),
            pltpu.VMEM((T, 2 * T), bf16),
            pltpu.VMEM((T, T), bf16),
            pltpu.VMEM((DSA_HEADS, KV_AUG, T), f32),
            pltpu.VMEM((3, T, T), f32),
            pltpu.VMEM((3, T, T), f32),
            pltpu.VMEM((3, 2 * T, T), bf16),
            pltpu.VMEM((3, T, T), f32),
            pltpu.VMEM((3, T, T), bf16),
            pltpu.VMEM((SB_HEADS, HEAD_DIM, T), f32),
            pltpu.VMEM((MIX_W, T), f32),
        ]
        out = pl.pallas_call(
            functools.partial(_attn_kernel, T=T, topk=topk, qi=qi),
            grid=(B,), in_specs=in_specs,
            out_specs=pl.BlockSpec((None, T, D), qrow),
            out_shape=jax.ShapeDtypeStruct((B, S, D), f32),
            scratch_shapes=scratch, input_output_aliases={GATE_OPERAND: 0}, name=f"hybrid_attn_q{qi}",
            compiler_params=pltpu.CompilerParams(
                dimension_semantics=("arbitrary",), vmem_limit_bytes=VMEM_LIMIT_BYTES),
        )(*args)
    return out


def _tile_sizes(S):
    T = 256
    assert S % T == 0 and T >= MAX_DISTANCE
    tm = 512 if S % 512 == 0 else T
    return tm, T


def kernel(x, mem, pre_norm_g, post_norm_g, w_in, w_uk, w_uv, kv_norm_g, w_mem_kv, w_out, rel_bias):
    B, S, D = x.shape
    assert D == D_MODEL and mem.shape[1] == N_MEM
    topk = min(TOPK_MAX, S // 4)
    tm, T = _tile_sizes(S)
    bias = _bias_tiles(rel_bias, T)
    for layer in range(w_in.shape[0]):
        w_packed = _pack_w_in(w_in[layer])
        wuk_t = jnp.transpose(w_uk[layer], (1, 2, 0)).astype(bf16)
        wuvt_h = jnp.transpose(w_uv[layer], (1, 2, 0)).astype(bf16)
        proj = _project(x, pre_norm_g[layer], w_packed, wuk_t, kv_norm_g[layer], tm, T)
        memk, memvt = _mem_kv(mem, w_mem_kv[layer].astype(bf16))
        x = _attention(x, proj, memk, memvt, wuvt_h, w_out[layer].astype(bf16), post_norm_g[layer],
                       bias, T, topk)
    return x
```

```python
import functools
import math

import jax
import jax.numpy as jnp
from jax import lax
from jax.experimental import pallas as pl
from jax.experimental.pallas import tpu as pltpu

D_MODEL = 1024
N_MEM = 256
HEAD_DIM = 64
SB_HEADS = 6
DSA_HEADS = 6
MEM_HEADS = 4
SB_W = SB_HEADS * HEAD_DIM
DSA_W = DSA_HEADS * HEAD_DIM
MEM_W = MEM_HEADS * HEAD_DIM
MIX_W = SB_W + DSA_W + MEM_W
KV_RANK = 128
IDX_HEADS = 8
IDX_DIM = 32
TOPK_MAX = 256
N_BUCKETS = 32
MAX_DISTANCE = 128
RMS_EPS = 1e-6

LANES = 128
SUBLANES = 8
BF16_ROWS = 16
ATTN_SCALE = HEAD_DIM ** -0.5
IDX_SCALE = (IDX_HEADS * IDX_DIM) ** -0.5
INT_MIN = -2 ** 31
NEG_BIG = -1e30
EXP_ZERO_BELOW = -105.0
VMEM_LIMIT_BYTES = 56 * 1024 * 1024
KV_AUG = KV_RANK + BF16_ROWS

OFF_SBQ = 0
OFF_SBK = OFF_SBQ + SB_W
OFF_SBV = OFF_SBK + SB_W
OFF_DSAQ = OFF_SBV + SB_W
OFF_CKV = OFF_DSAQ + DSA_W
OFF_IQ = OFF_CKV + KV_RANK
OFF_IKW = OFF_IQ + IDX_HEADS * IDX_DIM
OFF_MEMQ = OFF_IKW + LANES
OFF_GATE = OFF_MEMQ + MEM_W
PACKED_COLS = OFF_GATE + MIX_W

f32 = jnp.float32
bf16 = jnp.bfloat16
NT = (((1,), (1,)), ((), ()))


def _pack_w_in(w):
    o = 0
    parts = {}
    for name, n in (("sbq", SB_W), ("sbk", SB_W), ("sbv", SB_W), ("sbg", SB_W), ("dsaq", DSA_W),
                    ("ckv", KV_RANK), ("dsag", DSA_W), ("iq", IDX_HEADS * IDX_DIM), ("ik", IDX_DIM),
                    ("iw", IDX_HEADS), ("memq", MEM_W), ("memg", MEM_W)):
        parts[name] = w[:, o:o + n]
        o += n
    pad = jnp.zeros((w.shape[0], LANES - IDX_DIM - IDX_HEADS), w.dtype)
    packed = jnp.concatenate(
        [parts["sbq"], parts["sbk"], parts["sbv"], parts["dsaq"], parts["ckv"], parts["iq"],
         parts["ik"], parts["iw"], pad, parts["memq"], parts["sbg"], parts["dsag"], parts["memg"]], axis=1)
    assert packed.shape[1] == PACKED_COLS
    return packed.astype(bf16)


def _proj_kernel(x_ref, g_ref, w_ref, wuk_ref, kvg_ref,
                 sbq_ref, sbk_ref, sbvt_ref, qlat_ref, ckv_ref, ckvt_ref, iq_ref, idxk_ref, iwt_ref,
                 memq_ref, gate_ref, *, T):
    tm = x_ref.shape[0]
    x = x_ref[...]
    ms = jnp.mean(x * x, axis=-1, keepdims=True)
    h = (x * lax.rsqrt(ms + RMS_EPS) * g_ref[...]).astype(bf16)

    def seg(off, n):
        return jnp.dot(h, w_ref[:, off:off + n], preferred_element_type=f32)

    a = seg(OFF_SBQ, 2 * SB_W)
    for hd in range(SB_HEADS):
        lo = hd * HEAD_DIM
        sbq_ref[hd] = (a[:, lo:lo + HEAD_DIM] * ATTN_SCALE).astype(bf16)
        sbk_ref[hd] = a[:, SB_W + lo:SB_W + lo + HEAD_DIM].astype(bf16)

    vt = seg(OFF_SBV, SB_W).T
    for r in range(tm // T):
        for hd in range(SB_HEADS):
            sbvt_ref[r, hd] = vt[hd * HEAD_DIM:(hd + 1) * HEAD_DIM, r * T:(r + 1) * T].astype(bf16)

    dq = seg(OFF_DSAQ, DSA_W)
    for hd in range(DSA_HEADS):
        q = dq[:, hd * HEAD_DIM:(hd + 1) * HEAD_DIM].astype(bf16)
        ql = jnp.dot(q, wuk_ref[hd], preferred_element_type=f32)
        qlat_ref[hd] = (ql * ATTN_SCALE).astype(bf16)

    c = seg(OFF_CKV, KV_RANK)
    cms = jnp.mean(c * c, axis=-1, keepdims=True)
    cn = c * lax.rsqrt(cms + RMS_EPS) * kvg_ref[...]
    ckv_ref[...] = cn.astype(bf16)
    cnt = cn.T
    for r in range(tm // T):
        ckvt_ref[r, :KV_RANK, :] = cnt[:, r * T:(r + 1) * T].astype(bf16)
        ckvt_ref[r, KV_RANK:, :] = jnp.ones((BF16_ROWS, T), bf16)

    e = seg(OFF_IQ, IDX_HEADS * IDX_DIM)
    for hd in range(IDX_HEADS):
        iq_ref[hd] = e[:, hd * IDX_DIM:(hd + 1) * IDX_DIM].astype(bf16)

    kw = seg(OFF_IKW, LANES)
    idxk_ref[...] = kw[:, :IDX_DIM].astype(bf16)
    iwt_ref[...] = kw.T[IDX_DIM:IDX_DIM + IDX_HEADS, :] * IDX_SCALE

    mq = seg(OFF_MEMQ, MEM_W)
    for hd in range(MEM_HEADS):
        memq_ref[hd] = (mq[:, hd * HEAD_DIM:(hd + 1) * HEAD_DIM] * ATTN_SCALE).astype(bf16)

    gt = seg(OFF_GATE, MIX_W)
    gate_ref[...] = gt * jax.nn.sigmoid(gt)


def _project(x, pre_g, w_packed, wuk_t, kv_g, tm, T):
    B, S, D = x.shape
    grid = (B, S // tm)
    rpt = tm // T
    row = lambda b, i: (b, i, 0)
    head = lambda b, i: (b, 0, i, 0)
    const2 = lambda b, i: (0, 0)
    const3 = lambda b, i: (0, 0, 0)
    out_shape = (
        jax.ShapeDtypeStruct((B, SB_HEADS, S, HEAD_DIM), bf16),
        jax.ShapeDtypeStruct((B, SB_HEADS, S, HEAD_DIM), bf16),
        jax.ShapeDtypeStruct((B, S // T, SB_HEADS, HEAD_DIM, T), bf16),
        jax.ShapeDtypeStruct((B, DSA_HEADS, S, KV_RANK), bf16),
        jax.ShapeDtypeStruct((B, S, KV_RANK), bf16),
        jax.ShapeDtypeStruct((B, S // T, KV_AUG, T), bf16),
        jax.ShapeDtypeStruct((B, IDX_HEADS, S, IDX_DIM), bf16),
        jax.ShapeDtypeStruct((B, S, IDX_DIM), bf16),
        jax.ShapeDtypeStruct((B, IDX_HEADS, S), f32),
        jax.ShapeDtypeStruct((B, MEM_HEADS, S, HEAD_DIM), bf16),
        jax.ShapeDtypeStruct((B, S, MIX_W), f32),
    )
    out_specs = (
        pl.BlockSpec((None, SB_HEADS, tm, HEAD_DIM), head),
        pl.BlockSpec((None, SB_HEADS, tm, HEAD_DIM), head),
        pl.BlockSpec((None, rpt, SB_HEADS, HEAD_DIM, T), lambda b, i: (b, i, 0, 0, 0)),
        pl.BlockSpec((None, DSA_HEADS, tm, KV_RANK), head),
        pl.BlockSpec((None, tm, KV_RANK), row),
        pl.BlockSpec((None, rpt, KV_AUG, T), lambda b, i: (b, i, 0, 0)),
        pl.BlockSpec((None, IDX_HEADS, tm, IDX_DIM), head),
        pl.BlockSpec((None, tm, IDX_DIM), row),
        pl.BlockSpec((None, IDX_HEADS, tm), lambda b, i: (b, 0, i)),
        pl.BlockSpec((None, MEM_HEADS, tm, HEAD_DIM), head),
        pl.BlockSpec((None, tm, MIX_W), row),
    )
    in_specs = [
        pl.BlockSpec((None, tm, D), row),
        pl.BlockSpec((1, D), const2),
        pl.BlockSpec((D, PACKED_COLS), const2),
        pl.BlockSpec((DSA_HEADS, HEAD_DIM, KV_RANK), const3),
        pl.BlockSpec((1, KV_RANK), const2),
    ]
    return pl.pallas_call(
        functools.partial(_proj_kernel, T=T),
        grid=grid, in_specs=in_specs, out_specs=out_specs, out_shape=out_shape,
        name="in_proj",
        compiler_params=pltpu.CompilerParams(
            dimension_semantics=("arbitrary", "arbitrary"), vmem_limit_bytes=VMEM_LIMIT_BYTES),
    )(x, pre_g.reshape(1, D), w_packed, wuk_t, kv_g.reshape(1, KV_RANK))


def _memkv_kernel(mem_ref, w_ref, k_ref, vt_ref):
    m = mem_ref[...].astype(bf16)
    kv = jnp.dot(m, w_ref[...], preferred_element_type=f32)
    vt = kv[:, MEM_W:].T
    for hd in range(MEM_HEADS):
        lo = hd * HEAD_DIM
        k_ref[hd] = kv[:, lo:lo + HEAD_DIM].astype(bf16)
        vt_ref[hd] = vt[lo:lo + HEAD_DIM, :].astype(bf16)


def _mem_kv(mem, w_mem_kv_bf):
    B, M, D = mem.shape
    out_shape = (jax.ShapeDtypeStruct((B, MEM_HEADS, M, HEAD_DIM), bf16),
                 jax.ShapeDtypeStruct((B, MEM_HEADS, HEAD_DIM, M), bf16))
    return pl.pallas_call(
        _memkv_kernel, grid=(B,),
        in_specs=[pl.BlockSpec((None, M, D), lambda b: (b, 0, 0)),
                  pl.BlockSpec((D, 2 * MEM_W), lambda b: (0, 0))],
        out_specs=(pl.BlockSpec((None, MEM_HEADS, M, HEAD_DIM), lambda b: (b, 0, 0, 0)),
                   pl.BlockSpec((None, MEM_HEADS, HEAD_DIM, M), lambda b: (b, 0, 0, 0))),
        out_shape=out_shape, name="mem_kv",
        compiler_params=pltpu.CompilerParams(
            dimension_semantics=("arbitrary",), vmem_limit_bytes=VMEM_LIMIT_BYTES),
    )(mem, w_mem_kv_bf)


def _t5_bucket(n):
    max_exact = N_BUCKETS // 2
    nf = jnp.maximum(n, 1).astype(f32)
    large = max_exact + (jnp.log(nf / max_exact) / math.log(MAX_DISTANCE / max_exact)
                         * (N_BUCKETS - max_exact)).astype(jnp.int32)
    large = jnp.minimum(large, N_BUCKETS - 1)
    return jnp.where(n < max_exact, n, large)


def _bias_kernel(relb_ref, bias_ref, *, T):
    key_l = lax.broadcasted_iota(jnp.int32, (T, T), 0)
    qry_l = lax.broadcasted_iota(jnp.int32, (T, T), 1)
    for kind in range(3):
        bucket = _t5_bucket(jnp.maximum(kind * T + qry_l - key_l, 0))
        for hd in range(DSA_HEADS):
            tile = jnp.zeros((T, T), f32)
            for k in range(N_BUCKETS):
                tile = jnp.where(bucket == k, relb_ref[k, hd], tile)
            bias_ref[kind, hd] = tile


def _bias_tiles(rel_bias, T):
    return pl.pallas_call(
        functools.partial(_bias_kernel, T=T),
        in_specs=[pl.BlockSpec(memory_space=pltpu.SMEM)],
        out_shape=jax.ShapeDtypeStruct((3, DSA_HEADS, T, T), f32), name="t5_bias",
        compiler_params=pltpu.CompilerParams(vmem_limit_bytes=VMEM_LIMIT_BYTES),
    )(rel_bias)


def _fold_keys(v, op):
    t = v.shape[0]
    v3 = v.reshape(t // SUBLANES, SUBLANES, v.shape[1])
    return op(v3, axis=0)


def _loop(lo, hi, body, init):
    if isinstance(lo, int) and isinstance(hi, int):
        val = init
        for i in range(lo, hi):
            val = body(i, val)
        return val
    return lax.fori_loop(lo, hi, body, init)


def _fold_rows16(m):
    parts = [m[i * BF16_ROWS:(i + 1) * BF16_ROWS] for i in range(m.shape[0] // BF16_ROWS)]
    while len(parts) > 1:
        parts = [parts[i] + parts[i + 1] for i in range(0, len(parts), 2)]
    return parts[0]


def _attn_kernel(bias_ref, x_ref, sbq_ref, sbk_ref, sbvt_ref, qlat_ref, ckv_ref, ckvt_ref, iq_ref, idxk_ref,
                 iwt_ref, memq_ref, memk_ref, memvt_ref, gate_ref, wuvt_ref, wout_ref, postg_ref,
                 out_ref,
                 keys_ref, k16_ref, lo16_ref, lg_ref, tri2_ref, tri_lt_ref, acc_ref,
                 z_ref, lb_ref, pl_ref, cum_ref, w_ref, sbacc_ref, mixt_ref,
                 *, T, topk, qi):
    b = pl.program_id(0)
    nchunk = qi + 1
    i16 = jnp.int16
    key_l = lax.broadcasted_iota(jnp.int32, (T, T), 0)
    qry_l = lax.broadcasted_iota(jnp.int32, (T, T), 1)

    @pl.when(b == 0)
    def _init():
        later = jnp.where(qry_l > key_l, 1.0, 0.0).astype(bf16)
        tri2_ref[:, :T] = later
        tri2_ref[:, T:] = later
        tri_lt_ref[...] = jnp.where(qry_l < key_l, 1.0, 0.0).astype(bf16)

    iq2d = iq_ref[...].reshape(IDX_HEADS * T, IDX_DIM)
    iwt = iwt_ref[...]

    def score_body(j, carry):
        ks = j * T
        dots = lax.dot_general(idxk_ref[pl.ds(ks, T), :], iq2d, NT, preferred_element_type=f32)
        score = jnp.zeros((T, T), f32)
        for hd in range(IDX_HEADS):
            score = score + iwt[hd:hd + 1, :] * jnp.maximum(dots[:, hd * T:(hd + 1) * T], 0.0)
        bits = pltpu.bitcast(score, jnp.int32)
        key = jnp.where(bits < 0, INT_MIN - bits, bits)
        if j == qi:
            key = jnp.where(key_l <= qry_l, key, INT_MIN)
        keys_ref[j] = key
        k16_ref[j] = (key >> 16).astype(i16)
        return carry

    _loop(0, nchunk, score_body, 0)

    def count16(ref, cand, strict=False):
        c16 = cand.astype(i16)

        def body(j, acc):
            k = ref[j]
            hit = (k > c16) if strict else (k >= c16)
            return acc + _fold_rows16(jnp.where(hit, i16(1), i16(0)))

        acc = _loop(0, nchunk, body, jnp.zeros((BF16_ROWS, T), i16))
        return jnp.sum(acc.astype(f32), axis=0, keepdims=True)

    kf = float(topk)
    half = 1 << 15
    n_all = float(nchunk * T)
    n_pos = count16(k16_ref, jnp.zeros((1, T), jnp.int32))
    hi0 = jnp.where(n_pos >= kf, 0, -half).astype(jnp.int32)
    n0 = jnp.where(n_pos >= kf, n_pos, n_all)

    def hi_body(i, carry):
        hi, n_hi = carry
        cand = hi | (jnp.int32(1) << (14 - i))
        n = count16(k16_ref, cand)
        ok = n >= kf
        return jnp.where(ok, cand, hi), jnp.where(ok, n, n_hi)

    thr_hi, n_hi = lax.fori_loop(0, 15, hi_body, (hi0, n0))
    n_above = count16(k16_ref, thr_hi, strict=True)

    def lo_fill(j, carry):
        key = keys_ref[j]
        low = (key & 0xFFFF) - half
        lo16_ref[j] = jnp.where((key >> 16) == thr_hi, low, -half).astype(i16)
        return carry

    _loop(0, nchunk, lo_fill, 0)

    def lo_body(i, carry):
        lo, n_lo = carry
        cand = lo | (jnp.int32(1) << (15 - i))
        n = n_above + count16(lo16_ref, cand - half)
        ok = n >= kf
        return jnp.where(ok, cand, lo), jnp.where(ok, n, n_lo)

    thr_lo, n_ge = lax.fori_loop(0, 16, lo_body, (jnp.zeros((1, T), jnp.int32), n_hi))
    thr = thr_hi * (1 << 16) + thr_lo
    has_ties = jnp.max(jnp.where((n_ge > kf) & (thr > INT_MIN), 1.0, 0.0)) > 0.0
    thr_floor = jnp.maximum(thr, INT_MIN + 1)

    def count_above_thr(_):
        def body(j, acc):
            return acc + _fold_keys(jnp.where(keys_ref[j] > thr, 1.0, 0.0), jnp.sum)
        acc = _loop(0, nchunk, body, jnp.zeros((SUBLANES, T), f32))
        return jnp.sum(acc, axis=0, keepdims=True)

    need = kf - lax.cond(has_ties, count_above_thr, lambda _: jnp.zeros((1, T), f32), 0)

    def dsa_logits_walk(with_ties):
        def dsa_logits(j, carry):
            eq_seen, maxes = carry
            k = keys_ref[j]
            if with_ties:
                eq = jnp.where(k == thr, 1.0, 0.0)
                prefix = jnp.dot(tri_lt_ref[...], eq.astype(bf16), preferred_element_type=f32) + eq_seen
                bump = jnp.where(prefix >= need, 1, 0).astype(jnp.int32)
                sel = k >= jnp.maximum(thr + bump, INT_MIN + 1)
                eq_seen = eq_seen + jnp.sum(_fold_keys(eq, jnp.sum), axis=0, keepdims=True)
            else:
                sel = k >= thr_floor
            ckv = ckv_ref[pl.ds(j * T, T), :]
            kind = min(qi - j, 2)

            def head_logits(hd):
                return lax.dot_general(ckv, qlat_ref[hd], NT, preferred_element_type=f32)

            new_maxes = []
            ahead = head_logits(0)
            for hd in range(DSA_HEADS):
                logits = ahead
                if hd + 1 < DSA_HEADS:
                    ahead = head_logits(hd + 1)
                lg = jnp.where(sel, logits + bias_ref[kind, hd], NEG_BIG)
                lg_ref[j, :, hd * T:(hd + 1) * T] = lg
                new_maxes.append(jnp.maximum(maxes[hd], _fold_keys(lg, jnp.max)))
            return eq_seen, tuple(new_maxes)

        def walk(init_max):
            return _loop(0, nchunk, dsa_logits, (jnp.zeros((1, T), f32), init_max))[1]
        return walk

    init_max = tuple(jnp.full((SUBLANES, T), NEG_BIG, f32) for _ in range(DSA_HEADS))
    maxes = lax.cond(has_ties, dsa_logits_walk(True), dsa_logits_walk(False), init_max)
    row_max = [jnp.max(m, axis=0, keepdims=True) for m in maxes]

    acc_ref[...] = jnp.zeros(acc_ref.shape, f32)

    def dsa_values(j, carry):
        ckvt = ckvt_ref[j]
        for hd in range(DSA_HEADS):
            p = jnp.exp(lg_ref[j, :, hd * T:(hd + 1) * T] - row_max[hd]).astype(bf16)
            acc_ref[hd] += jnp.dot(ckvt, p, preferred_element_type=f32)
        return carry

    _loop(0, nchunk, dsa_values, 0)

    for hd in range(DSA_HEADS):
        a = acc_ref[hd]
        o_lat = (a[:KV_RANK] / a[KV_RANK:KV_RANK + 1]).astype(bf16)
        mixt_ref[SB_W + hd * HEAD_DIM:SB_W + (hd + 1) * HEAD_DIM, :] = jnp.dot(
            wuvt_ref[hd], o_lat, preferred_element_type=f32)

    def mem_logits(hd):
        return lax.dot_general(memk_ref[hd], memq_ref[hd], NT, preferred_element_type=f32)

    ahead = mem_logits(0)
    for hd in range(MEM_HEADS):
        lg = ahead
        if hd + 1 < MEM_HEADS:
            ahead = mem_logits(hd + 1)
        e = jnp.exp(lg - jnp.max(lg, axis=0, keepdims=True))
        p = e / jnp.sum(e, axis=0, keepdims=True)
        lo = SB_W + DSA_W + hd * HEAD_DIM
        mixt_ref[lo:lo + HEAD_DIM, :] = jnp.dot(memvt_ref[hd], p.astype(bf16), preferred_element_type=f32)

    sbacc_ref[...] = jnp.zeros(sbacc_ref.shape, f32)
    RB = 64
    SB_SLOTS = z_ref.shape[0]
    sign_bit = jnp.int32(INT_MIN)

    def causal_rows(r0):
        return tri2_ref[r0:r0 + RB, :T].astype(f32)

    def sb_scores(j, hd, s):
        ks = j * T if isinstance(j, int) else pl.multiple_of(j * T, T)
        z_ref[s] = lax.dot_general(sbk_ref[hd, pl.ds(ks, T), :], sbq_ref[hd], NT, preferred_element_type=f32)

    def sb_logs(s, masked):
        first = None
        for r0 in range(0, T, RB):
            z = z_ref[s, r0:r0 + RB, :]
            neg_abs = pltpu.bitcast(pltpu.bitcast(z, jnp.int32) | sign_bit, f32)
            p = jnp.maximum(z, 0.0) + jnp.log(1.0 + jnp.exp(neg_abs))
            lb_ref[s, r0:r0 + RB, :] = z - p
            if masked:
                p = p * causal_rows(r0)
            hi = p.astype(bf16)
            pl_ref[s, r0:r0 + RB, :] = hi
            pl_ref[s, T + r0:T + r0 + RB, :] = (p - hi.astype(f32)).astype(bf16)
            if r0 == 0:
                first = p[0:1, :]
        return first

    def sb_suffix(s):
        cum_ref[s] = jnp.dot(tri2_ref[...], pl_ref[s], preferred_element_type=f32)

    def sb_weights(s, carry, first, masked):
        for r0 in range(0, T, RB):
            w = jnp.exp(lb_ref[s, r0:r0 + RB, :] - cum_ref[s, r0:r0 + RB, :] + carry)
            if masked:
                w = w * causal_rows(r0)
            w_ref[s, r0:r0 + RB, :] = w.astype(bf16)
        return carry - (cum_ref[s, 0:1, :] + first)

    def sb_values(j, hd, s):
        sbacc_ref[hd] += jnp.dot(sbvt_ref[j, hd], w_ref[s], preferred_element_type=f32)

    def sb_chunks(chunks, carries):
        steps = [(j, masked, hd) for (j, masked) in chunks for hd in range(SB_HEADS)]
        n = len(steps)
        carries = list(carries)
        firsts = [None] * n

        def scores(k):
            sb_scores(steps[k][0], steps[k][2], k % SB_SLOTS)

        def finish(k):
            j, masked, hd = steps[k]
            carries[hd] = sb_weights(k % SB_SLOTS, carries[hd], firsts[k], masked)
            sb_values(j, hd, k % SB_SLOTS)

        for k in range(min(2, n)):
            scores(k)
        for t in range(n + 2):
            if t + 2 < n:
                scores(t + 2)
            if t < n:
                firsts[t] = sb_logs(t % SB_SLOTS, steps[t][1])
                sb_suffix(t % SB_SLOTS)
            if 0 <= t - 2 < n:
                finish(t - 2)
        return tuple(carries)

    zero_carries = tuple(jnp.zeros((1, T), f32) for _ in range(SB_HEADS))
    first_chunks = [(qi, True), (qi - 1, False)] if qi >= 1 else [(qi, True)]
    carries = sb_chunks(first_chunks, zero_carries)

    def carry_max(cs):
        return jnp.max(functools.reduce(jnp.maximum, cs))

    def sb_cond(state):
        i, cmax, _ = state
        return (i < nchunk) & (cmax >= EXP_ZERO_BELOW)

    def sb_body(state):
        i, _, cs = state
        cs = sb_chunks([(qi - i, False)], cs)
        return i + 1, carry_max(cs), cs

    if nchunk > 2:
        lax.while_loop(sb_cond, sb_body, (jnp.int32(2), carry_max(carries), carries))
    for hd in range(SB_HEADS):
        mixt_ref[hd * HEAD_DIM:(hd + 1) * HEAD_DIM, :] = sbacc_ref[hd]

    gated = (mixt_ref[...].T * gate_ref[...]).astype(bf16)
    y = jnp.dot(gated, wout_ref[...], preferred_element_type=f32)
    ms = jnp.mean(y * y, axis=-1, keepdims=True)
    out_ref[...] = x_ref[...] + y * lax.rsqrt(ms + RMS_EPS) * postg_ref[...]


GATE_OPERAND = 14


def _attention(x, proj, memk, memvt, wuvt_h, wout_bf, post_g, bias, T, topk):
    sbq, sbk, sbvt, qlat, ckv, ckvt, iq, idxk, iwt, memq, gate = proj
    B, S, D = x.shape
    assert gate.shape == (B, S, D) and gate.dtype == x.dtype
    nq = S // T
    out = gate
    for qi in range(nq):
        nk = qi + 1
        qrow = lambda b, qi=qi: (b, qi, 0)
        qhead = lambda b, qi=qi: (b, 0, qi, 0)
        krow = lambda b: (b, 0, 0)
        khead = lambda b: (b, 0, 0, 0)
        in_specs = [
            pl.BlockSpec((3, DSA_HEADS, T, T), lambda b: (0, 0, 0, 0)),
            pl.BlockSpec((None, T, D), qrow),
            pl.BlockSpec((None, SB_HEADS, T, HEAD_DIM), qhead),
            pl.BlockSpec((None, SB_HEADS, nk * T, HEAD_DIM), khead),
            pl.BlockSpec((None, nk, SB_HEADS, HEAD_DIM, T), lambda b: (b, 0, 0, 0, 0)),
            pl.BlockSpec((None, DSA_HEADS, T, KV_RANK), qhead),
            pl.BlockSpec((None, nk * T, KV_RANK), krow),
            pl.BlockSpec((None, nk, KV_AUG, T), khead),
            pl.BlockSpec((None, IDX_HEADS, T, IDX_DIM), qhead),
            pl.BlockSpec((None, nk * T, IDX_DIM), krow),
            pl.BlockSpec((None, IDX_HEADS, T), lambda b, qi=qi: (b, 0, qi)),
            pl.BlockSpec((None, MEM_HEADS, T, HEAD_DIM), qhead),
            pl.BlockSpec((None, MEM_HEADS, N_MEM, HEAD_DIM), khead),
            pl.BlockSpec((None, MEM_HEADS, HEAD_DIM, N_MEM), khead),
            pl.BlockSpec((None, T, MIX_W), qrow),
            pl.BlockSpec((DSA_HEADS, HEAD_DIM, KV_RANK), lambda b: (0, 0, 0)),
            pl.BlockSpec((MIX_W, D), lambda b: (0, 0)),
            pl.BlockSpec((1, D), lambda b: (0, 0)),
        ]
        args = [bias, x, sbq, sbk, sbvt, qlat, ckv, ckvt, iq, idxk, iwt, memq, memk, memvt, out,
                wuvt_h, wout_bf, post_g.reshape(1, D)]
        assert args[GATE_OPERAND] is out
        scratch = [
            pltpu.VMEM((nk, T, T), jnp.int32),
            pltpu.VMEM((nk, T, T), jnp.int16),
            pltpu.VMEM((nk, T, T), jnp.int16),
            pltpu.VMEM((nk, T, DSA_HEADS * T), f32),
            pltpu.VMEM((T, 2 * T), bf16),
            pltpu.VMEM((T, T), bf16),
            pltpu.VMEM((DSA_HEADS, KV_AUG, T), f32),
            pltpu.VMEM((3, T, T), f32),
            pltpu.VMEM((3, T, T), f32),
            pltpu.VMEM((3, 2 * T, T), bf16),
            pltpu.VMEM((3, T, T), f32),
            pltpu.VMEM((3, T, T), bf16),
            pltpu.VMEM((SB_HEADS, HEAD_DIM, T), f32),
            pltpu.VMEM((MIX_W, T), f32),
        ]
        out = pl.pallas_call(
            functools.partial(_attn_kernel, T=T, topk=topk, qi=qi),
            grid=(B,), in_specs=in_specs,
            out_specs=pl.BlockSpec((None, T, D), qrow),
            out_shape=jax.ShapeDtypeStruct((B, S, D), f32),
            scratch_shapes=scratch, input_output_aliases={GATE_OPERAND: 0}, name=f"hybrid_attn_q{qi}",
            compiler_params=pltpu.CompilerParams(
                dimension_semantics=("arbitrary",), vmem_limit_bytes=VMEM_LIMIT_BYTES),
        )(*args)
    return out


def _tile_sizes(S):
    T = 256
    assert S % T == 0 and T >= MAX_DISTANCE
    tm = 1024 if S % 1024 == 0 else T
    return tm, T


def kernel(x, mem, pre_norm_g, post_norm_g, w_in, w_uk, w_uv, kv_norm_g, w_mem_kv, w_out, rel_bias):
    B, S, D = x.shape
    assert D == D_MODEL and mem.shape[1] == N_MEM
    topk = min(TOPK_MAX, S // 4)
    tm, T = _tile_sizes(S)
    bias = _bias_tiles(rel_bias, T)
    for layer in range(w_in.shape[0]):
        w_packed = _pack_w_in(w_in[layer])
        wuk_t = jnp.transpose(w_uk[layer], (1, 2, 0)).astype(bf16)
        wuvt_h = jnp.transpose(w_uv[layer], (1, 2, 0)).astype(bf16)
        proj = _project(x, pre_norm_g[layer], w_packed, wuk_t, kv_norm_g[layer], tm, T)
        memk, memvt = _mem_kv(mem, w_mem_kv[layer].astype(bf16))
        x = _attention(x, proj, memk, memvt, wuvt_h, w_out[layer].astype(bf16), post_norm_g[layer],
                       bias, T, topk)
    return x
```
